```python
import jax, jax.numpy as jnp
from jax import lax
import numpy as np

D_MODEL = 1024
BATCH = 2
SEQ = 8192
DEPTH = 1

CHUNK = 64
Q_BLOCK = 128
ROPE_THETA = 500000.0
EPS = 1e-6
NEG = -1e30

MLA_HEADS = 8
MLA_Q_LORA = 384
MLA_KV_LORA = 256
MLA_NOPE = 64
MLA_ROPE = 32
MLA_V = 64

DSA_HEADS = 8
DSA_HEAD_DIM = 64
DSA_ROT = DSA_HEAD_DIM // 4
IDX_HEADS = 8
IDX_DIM = 64
IDX_ROT = IDX_DIM // 4
DSA_TOPK_MAX = 256

N_GROUPS = 8
EXPERTS_PER_GROUP = 4
N_EXPERTS = N_GROUPS * EXPERTS_PER_GROUP
EXPERT_FF = 256
TOPK_IN_GROUP = 2

IN_SIZES = (
    MLA_Q_LORA,
    MLA_KV_LORA,
    MLA_ROPE,
    DSA_HEADS * DSA_HEAD_DIM,
    DSA_HEADS * DSA_HEAD_DIM,
    DSA_HEADS * DSA_HEAD_DIM,
    IDX_HEADS * IDX_DIM,
    IDX_DIM,
    IDX_HEADS,
    D_MODEL,
    D_MODEL,
)
IN_WIDTH = sum(IN_SIZES)

kernel_name = 'hybrid_mla_dsa_hier_moe_block'


def rmsnorm(x, g):
    xf = x.astype(jnp.float32)
    y = xf * lax.rsqrt(jnp.mean(xf * xf, axis=-1, keepdims=True) + EPS)
    return (y * g.astype(jnp.float32)).astype(x.dtype)


def rope(x, pos, rot_dim):
    half = rot_dim // 2
    inv_freq = ROPE_THETA ** (-jnp.arange(half, dtype=jnp.float32) * 2.0 / rot_dim)
    ang = pos.astype(jnp.float32)[:, None] * inv_freq[None, :]
    cos = jnp.cos(ang)[:, None, :]
    sin = jnp.sin(ang)[:, None, :]
    xr = x[..., :rot_dim].astype(jnp.float32)
    x1, x2 = xr[..., :half], xr[..., half:]
    rot = jnp.concatenate([x1 * cos - x2 * sin, x2 * cos + x1 * sin], axis=-1).astype(x.dtype)
    return jnp.concatenate([rot, x[..., rot_dim:]], axis=-1)


def _to_blocks(a):
    b, s = a.shape[0], a.shape[1]
    return a.reshape(b, s // Q_BLOCK, Q_BLOCK, *a.shape[2:]).swapaxes(0, 1)


def _from_blocks(a):
    a = a.swapaxes(0, 1)
    return a.reshape(a.shape[0], a.shape[1] * a.shape[2], *a.shape[3:])


def chunk_causal_attention(q, k, v, scale):
    s_len = q.shape[1]
    nb = s_len // Q_BLOCK
    key_chunk = jnp.arange(s_len) // CHUNK

    def one_block(args):
        qi, bi = args
        q_chunk = (bi * Q_BLOCK + jnp.arange(Q_BLOCK)) // CHUNK
        mask = key_chunk[None, :] <= q_chunk[:, None]
        s = jnp.einsum('bqhd,bkhd->bhqk', qi, k).astype(jnp.float32) * scale
        s = jnp.where(mask[None, None], s, NEG)
        p = jax.nn.softmax(s, axis=-1).astype(v.dtype)
        return jnp.einsum('bhqk,bkhd->bqhd', p, v)

    out = lax.map(one_block, (_to_blocks(q), jnp.arange(nb)))
    return _from_blocks(out)


def indexed_sparse_attention(q, k, v, q_idx, k_idx, w_idx, top_k):
    s_len = q.shape[1]
    nb = s_len // Q_BLOCK
    key_chunk = jnp.arange(s_len) // CHUNK
    gather = jax.vmap(lambda a, i: a[i])

    def one_block(args):
        qi, qii, wi, bi = args
        q_chunk = (bi * Q_BLOCK + jnp.arange(Q_BLOCK)) // CHUNK
        adm = key_chunk[None, :] <= q_chunk[:, None]
        logits = jnp.einsum('bqhd,bkd->bqhk', qii, k_idx).astype(jnp.float32) * (IDX_DIM ** -0.5)
        score = jnp.einsum('bqh,bqhk->bqk', wi.astype(jnp.float32), jax.nn.relu(logits))
        score = jnp.where(adm[None], score, -jnp.inf)
        _, sel = lax.top_k(score, top_k)
        sel_ok = (sel // CHUNK) <= q_chunk[None, :, None]
        k_sel = gather(k, sel)
        v_sel = gather(v, sel)
        s = jnp.einsum('bqhd,bqkhd->bhqk', qi, k_sel).astype(jnp.float32) * (DSA_HEAD_DIM ** -0.5)
        s = jnp.where(sel_ok[:, None], s, NEG)
        p = jax.nn.softmax(s, axis=-1).astype(v.dtype)
        return jnp.einsum('bhqk,bqkhd->bqhd', p, v_sel)

    out = lax.map(one_block, (_to_blocks(q), _to_blocks(q_idx), _to_blocks(w_idx), jnp.arange(nb)))
    return _from_blocks(out)


def hierarchical_moe(h, w_router_group, b_router_group, w_router_expert, b_router_expert,
                     w_gate, w_up, w_down):
    t = h.shape[0]
    group_logits = jnp.matmul(h, w_router_group).astype(jnp.float32) + b_router_group.astype(jnp.float32)
    p_group = jax.nn.softmax(group_logits, axis=-1)
    g_sel = jnp.argmax(group_logits, axis=-1)
    w_grp = jnp.take_along_axis(p_group, g_sel[:, None], axis=-1)
    exp_logits = (jnp.matmul(h, w_router_expert).astype(jnp.float32)
                  + b_router_expert.astype(jnp.float32)).reshape(t, N_GROUPS, EXPERTS_PER_GROUP)
    in_group = jnp.take_along_axis(exp_logits, g_sel[:, None, None], axis=1)[:, 0]
    top_vals, top_idx = lax.top_k(in_group, TOPK_IN_GROUP)
    w_top = jax.nn.softmax(top_vals, axis=-1) * w_grp
    expert_id = g_sel[:, None] * EXPERTS_PER_GROUP + top_idx
    combine = jnp.sum(jax.nn.one_hot(expert_id, N_EXPERTS, dtype=jnp.float32) * w_top[..., None],
                      axis=1).astype(h.dtype)
    out = jnp.zeros_like(h)
    for e in range(N_EXPERTS):
        hid = jax.nn.silu(jnp.matmul(h, w_gate[e])) * jnp.matmul(h, w_up[e])
        out = out + jnp.matmul(hid, w_down[e]) * combine[:, e:e + 1]
    return out


def hybrid_layer(x, pos, norm_mix_g, w_in, mla_q_norm_g, w_uq, mla_kv_norm_g, w_uk, w_uv,
                 w_o_a, w_o_b, w_out, norm_ffn_g, w_router_group, b_router_group,
                 w_router_expert, b_router_expert, w_gate, w_up, w_down):
    b, s, d = x.shape
    h = rmsnorm(x, norm_mix_g)
    proj = jnp.einsum('bsd,de->bse', h, w_in)
    splits = np.cumsum(IN_SIZES)[:-1].tolist()
    (c_q, c_kv, k_r, q_b, k_b, v_b, q_i, k_i, w_i, g_a, g_b) = jnp.split(proj, splits, axis=-1)

    c_q = rmsnorm(c_q, mla_q_norm_g)
    q_a = jnp.einsum('bsr,re->bse', c_q, w_uq).reshape(b, s, MLA_HEADS, MLA_NOPE + MLA_ROPE)
    q_a = jnp.concatenate([q_a[..., :MLA_NOPE], rope(q_a[..., MLA_NOPE:], pos, MLA_ROPE)], axis=-1)
    c_kv = rmsnorm(c_kv, mla_kv_norm_g)
    k_nope = jnp.einsum('bsr,re->bse', c_kv, w_uk).reshape(b, s, MLA_HEADS, MLA_NOPE)
    v_a = jnp.einsum('bsr,re->bse', c_kv, w_uv).reshape(b, s, MLA_HEADS, MLA_V)
    k_pe = rope(k_r[:, :, None, :], pos, MLA_ROPE)
    k_a = jnp.concatenate([k_nope, jnp.broadcast_to(k_pe, (b, s, MLA_HEADS, MLA_ROPE))], axis=-1)
    o_a = chunk_causal_attention(q_a, k_a, v_a, (MLA_NOPE + MLA_ROPE) ** -0.5)
    y_a = jnp.einsum('bse,ed->bsd', o_a.reshape(b, s, MLA_HEADS * MLA_V), w_o_a)

    q_b = rope(q_b.reshape(b, s, DSA_HEADS, DSA_HEAD_DIM), pos, DSA_ROT)
    k_b = rope(k_b.reshape(b, s, DSA_HEADS, DSA_HEAD_DIM), pos, DSA_ROT)
    v_b = v_b.reshape(b, s, DSA_HEADS, DSA_HEAD_DIM)
    q_i = rope(q_i.reshape(b, s, IDX_HEADS, IDX_DIM), pos, IDX_ROT)
    k_i = rope(k_i[:, :, None, :], pos, IDX_ROT)[:, :, 0, :]
    top_k = min(DSA_TOPK_MAX, s // 4)
    o_b = indexed_sparse_attention(q_b, k_b, v_b, q_i, k_i, w_i * (IDX_HEADS ** -0.5), top_k)
    y_b = jnp.einsum('bse,ed->bsd', o_b.reshape(b, s, DSA_HEADS * DSA_HEAD_DIM), w_o_b)

    y = jax.nn.sigmoid(g_a) * y_a + jax.nn.sigmoid(g_b) * y_b
    x = x + jnp.einsum('bsd,de->bse', y, w_out)

    h2 = rmsnorm(x, norm_ffn_g).reshape(b * s, d)
    m = hierarchical_moe(h2, w_router_group, b_router_group, w_router_expert, b_router_expert,
                         w_gate, w_up, w_down)
    return x + m.reshape(b, s, d)


def setup_inputs(seed: int = 0) -> dict:
    key = jax.random.key(seed)
    ks = jax.random.split(key, 20)
    f32 = jnp.float32
    L = DEPTH

    def dense(k, shape, fan_in):
        return jax.random.normal(k, shape, f32) * (fan_in ** -0.5)

    def gain(k, shape):
        return 1.0 + 0.05 * jax.random.normal(k, shape, f32)

    return {
        'x': jax.random.normal(ks[0], (BATCH, SEQ, D_MODEL), f32),
        'norm_mix_g': gain(ks[1], (L, D_MODEL)),
        'w_in': dense(ks[2], (L, D_MODEL, IN_WIDTH), D_MODEL),
        'mla_q_norm_g': gain(ks[3], (L, MLA_Q_LORA)),
        'w_uq': dense(ks[4], (L, MLA_Q_LORA, MLA_HEADS * (MLA_NOPE + MLA_ROPE)), MLA_Q_LORA),
        'mla_kv_norm_g': gain(ks[5], (L, MLA_KV_LORA)),
        'w_uk': dense(ks[6], (L, MLA_KV_LORA, MLA_HEADS * MLA_NOPE), MLA_KV_LORA),
        'w_uv': dense(ks[7], (L, MLA_KV_LORA, MLA_HEADS * MLA_V), MLA_KV_LORA),
        'w_o_a': dense(ks[8], (L, MLA_HEADS * MLA_V, D_MODEL), MLA_HEADS * MLA_V),
        'w_o_b': dense(ks[9], (L, DSA_HEADS * DSA_HEAD_DIM, D_MODEL), DSA_HEADS * DSA_HEAD_DIM),
        'w_out': dense(ks[10], (L, D_MODEL, D_MODEL), D_MODEL),
        'norm_ffn_g': gain(ks[11], (L, D_MODEL)),
        'w_router_group': dense(ks[12], (L, D_MODEL, N_GROUPS), D_MODEL),
        'b_router_group': 0.01 * jax.random.normal(ks[13], (L, N_GROUPS), f32),
        'w_router_expert': dense(ks[14], (L, D_MODEL, N_EXPERTS), D_MODEL),
        'b_router_expert': 0.01 * jax.random.normal(ks[15], (L, N_EXPERTS), f32),
        'w_gate': dense(ks[16], (L, N_EXPERTS, D_MODEL, EXPERT_FF), D_MODEL),
        'w_up': dense(ks[17], (L, N_EXPERTS, D_MODEL, EXPERT_FF), D_MODEL),
        'w_down': dense(ks[18], (L, N_EXPERTS, EXPERT_FF, D_MODEL), EXPERT_FF),
        'final_norm_g': gain(ks[19], (D_MODEL,)),
    }


def reference(x, norm_mix_g, w_in, mla_q_norm_g, w_uq, mla_kv_norm_g, w_uk, w_uv, w_o_a, w_o_b,
              w_out, norm_ffn_g, w_router_group, b_router_group, w_router_expert, b_router_expert,
              w_gate, w_up, w_down, final_norm_g):
    pos = jnp.arange(x.shape[1], dtype=jnp.int32)
    for l in range(DEPTH):
        x = hybrid_layer(x, pos, norm_mix_g[l], w_in[l], mla_q_norm_g[l], w_uq[l], mla_kv_norm_g[l],
                         w_uk[l], w_uv[l], w_o_a[l], w_o_b[l], w_out[l], norm_ffn_g[l],
                         w_router_group[l], b_router_group[l], w_router_expert[l],
                         b_router_expert[l], w_gate[l], w_up[l], w_down[l])
    return rmsnorm(x, final_norm_g)
```

```python
import functools
import math

import numpy as np
import jax
import jax.numpy as jnp
from jax import lax
from jax.experimental import pallas as pl
from jax.experimental.pallas import tpu as pltpu

F32 = jnp.float32
BF16 = jnp.bfloat16
I32 = jnp.int32

D_MODEL = 1024
CHUNK = 64
CHUNK_SHIFT = 6
ROPE_THETA = 500000.0
EPS = 1e-6
NEG = -1e30
LOG2E = 1.4426950408889634

N_HEADS = 8
HEAD_V = 64
MLA_Q_LORA = 384
MLA_KV_LORA = 256
MLA_NOPE = 64
MLA_ROPE = 32
DSA_HEAD_DIM = 64
DSA_ROT = 16
IDX_HEADS = 8
IDX_DIM = 64
DSA_TOPK_MAX = 256

N_GROUPS = 8
EXPERTS_PER_GROUP = 4
N_EXPERTS = N_GROUPS * EXPERTS_PER_GROUP
EXPERT_FF = 256

LANES = 128
INT_MIN = -(2 ** 31)
VMEM_LIMIT = 52 * 1024 * 1024

_C_CQ, _C_CKV, _C_QB, _C_KB, _C_VB, _C_QI, _C_KI, _C_MISC, _C_END = (
    0, 384, 640, 1152, 1664, 2176, 2688, 2816, 2944)
_MISC_KR = 64
_MISC_W = 96


def _rms(x, g):
    return x * lax.rsqrt(jnp.mean(x * x, axis=-1, keepdims=True) + EPS) * g


def _tile_lanes(t, width):
    reps = width // LANES
    return t if reps == 1 else jnp.concatenate([t] * reps, axis=1)


def _rope(x, cos, sin_m, sin_p, half):
    w = x.shape[1]
    c, sm, sp = (_tile_lanes(t, w) for t in (cos, sin_m, sin_p))
    return x * c + pltpu.roll(x, w - half, 1) * sm + pltpu.roll(x, half, 1) * sp


def _inproj_kernel(x_ref, g_ref, w1_ref, wuq_ref, wuk_ref, wuv_ref, gq_ref, gkv_ref, tab_ref,
                   qa_ref, ka_ref, vaT_ref, qb_ref, kb_ref, vbT_ref, qi_ref, ki_ref, wT_ref,
                   *, scale_a):
    x = x_ref[0]
    h = _rms(x, g_ref[...]).astype(BF16)

    def proj(a, b):
        return jnp.dot(h, w1_ref[:, a:b], preferred_element_type=F32)

    cos_a, sin_am, sin_ap = tab_ref[0], tab_ref[1], tab_ref[2]
    cos_b, sin_bm, sin_bp = tab_ref[3], tab_ref[4], tab_ref[5]

    cq = _rms(proj(_C_CQ, _C_CKV), gq_ref[...]).astype(BF16)
    qa = jnp.dot(cq, wuq_ref[...], preferred_element_type=F32)
    qa = _rope(qa, cos_b, sin_bm, sin_bp, MLA_ROPE // 2) * scale_a
    ckv = _rms(proj(_C_CKV, _C_QB), gkv_ref[...]).astype(BF16)
    kn = jnp.dot(ckv, wuk_ref[...], preferred_element_type=F32)
    misc = _rope(proj(_C_MISC, _C_END), cos_b, sin_bm, sin_bp, MLA_ROPE // 2)
    lane = lax.broadcasted_iota(I32, misc.shape, 1)
    kpe = jnp.where((lane >= _MISC_KR) & (lane < _MISC_KR + MLA_ROPE), misc, 0.0)
    for hh in range(N_HEADS):
        sl = slice(hh * LANES, (hh + 1) * LANES)
        qa_ref[0, hh] = qa[:, sl].astype(BF16)
        ka_ref[0, hh] = (kn[:, sl] + kpe).astype(BF16)
    va = jnp.dot(ckv, wuv_ref[...], preferred_element_type=F32)
    vaT_ref[0] = va.T.astype(BF16)

    qb = _rope(proj(_C_QB, _C_KB), cos_a, sin_am, sin_ap, DSA_ROT // 2) * (DSA_HEAD_DIM ** -0.5 * LOG2E)
    kb = _rope(proj(_C_KB, _C_VB), cos_a, sin_am, sin_ap, DSA_ROT // 2)
    qi = _rope(proj(_C_QI, _C_KI), cos_a, sin_am, sin_ap, DSA_ROT // 2) * (IDX_DIM ** -0.5)
    for j in range(N_HEADS // 2):
        sl = slice(j * LANES, (j + 1) * LANES)
        qb_ref[0, j] = qb[:, sl].astype(BF16)
        kb_ref[0, j] = kb[:, sl].astype(BF16)
        qi_ref[0, j] = qi[:, sl].astype(BF16)
    vb = proj(_C_VB, _C_QI)
    vbT_ref[0] = vb.T.astype(BF16)
    ki2 = _rope(proj(_C_KI, _C_MISC), cos_a, sin_am, sin_ap, DSA_ROT // 2)
    ki_ref[0] = ki2.astype(BF16)
    wT_ref[0] = misc.T[_MISC_W:_MISC_W + IDX_HEADS, :] * (IDX_HEADS ** -0.5)


def _inproj_call(x, g_mix, w1, wuq, wuk, wuv, gq, gkv, tab, tm):
    b, s, d = x.shape
    nh, hp = N_HEADS, N_HEADS // 2
    full = lambda shape: pl.BlockSpec(shape, lambda bi, i: (0,) * len(shape))
    out_shape = (
        jax.ShapeDtypeStruct((b, nh, s, LANES), BF16),
        jax.ShapeDtypeStruct((b, nh, s, LANES), BF16),
        jax.ShapeDtypeStruct((b, nh * HEAD_V, s), BF16),
        jax.ShapeDtypeStruct((b, hp, s, LANES), BF16),
        jax.ShapeDtypeStruct((b, hp, s, LANES), BF16),
        jax.ShapeDtypeStruct((b, nh * HEAD_V, s), BF16),
        jax.ShapeDtypeStruct((b, hp, s, LANES), BF16),
        jax.ShapeDtypeStruct((b, s, LANES), BF16),
        jax.ShapeDtypeStruct((b, IDX_HEADS, s), F32),
    )
    head_spec = lambda n: pl.BlockSpec((1, n, tm, LANES), lambda bi, i: (bi, 0, i, 0))
    t_spec = pl.BlockSpec((1, nh * HEAD_V, tm), lambda bi, i: (bi, 0, i))
    out_specs = (head_spec(nh), head_spec(nh), t_spec, head_spec(hp), head_spec(hp), t_spec,
                 head_spec(hp), pl.BlockSpec((1, tm, LANES), lambda bi, i: (bi, i, 0)),
                 pl.BlockSpec((1, IDX_HEADS, tm), lambda bi, i: (bi, 0, i)))
    in_specs = [
        pl.BlockSpec((1, tm, d), lambda bi, i: (bi, i, 0)),
        full(g_mix.shape), full(w1.shape), full(wuq.shape), full(wuk.shape), full(wuv.shape),
        full(gq.shape), full(gkv.shape),
        pl.BlockSpec((tab.shape[0], tm, LANES), lambda bi, i: (0, i, 0)),
    ]
    scale_a = (MLA_NOPE + MLA_ROPE) ** -0.5 * LOG2E
    return pl.pallas_call(
        functools.partial(_inproj_kernel, scale_a=scale_a),
        out_shape=out_shape, grid=(b, s // tm), in_specs=in_specs, out_specs=out_specs,
        compiler_params=pltpu.CompilerParams(
            dimension_semantics=("arbitrary", "arbitrary"), vmem_limit_bytes=VMEM_LIMIT),
        name="inproj",
    )(x, g_mix, w1, wuq, wuk, wuv, gq, gkv, tab)


def _flash_head(h, q_ref, k_ref, vT_ref, m_scr, l_scr, acc_scr, pairs, mask):
    if pairs:
        j = lax.shift_right_logical(h, 1)
        q = q_ref[0, j]
        lane = lax.broadcasted_iota(I32, q.shape, 1)
        q = jnp.where(lax.shift_right_logical(lane, 6) == (h & 1), q, jnp.zeros_like(q))
        k = k_ref[0, j]
    else:
        q = q_ref[0, h]
        k = k_ref[0, h]
    s = lax.dot_general(k, q, (((1,), (1,)), ((), ())), preferred_element_type=F32)
    if mask is not None:
        s = jnp.where(mask, s, NEG)
    row = pl.ds(h, 1)
    m_prev = m_scr[row, :]
    m_new = jnp.maximum(m_prev, jnp.max(s, axis=0, keepdims=True))
    p = jnp.exp2(s - m_new)
    alpha = jnp.exp2(m_prev - m_new)
    l_scr[row, :] = alpha * l_scr[row, :] + jnp.sum(p, axis=0, keepdims=True)
    m_scr[row, :] = m_new
    rows = pl.ds(pl.multiple_of(h * HEAD_V, HEAD_V), HEAD_V)
    pv = jnp.dot(vT_ref[0, rows, :], p.astype(BF16), preferred_element_type=F32)
    acc_scr[rows, :] = alpha * acc_scr[rows, :] + pv


def _flash_init(m_scr, l_scr, acc_scr):
    m_scr[...] = jnp.full(m_scr.shape, NEG, F32)
    l_scr[...] = jnp.zeros(l_scr.shape, F32)
    acc_scr[...] = jnp.zeros(acc_scr.shape, F32)


def _flash_finalize(o_ref, l_scr, acc_scr):
    inv = 1.0 / l_scr[...]
    parts = [acc_scr[hh * HEAD_V:(hh + 1) * HEAD_V, :] * inv[hh:hh + 1, :] for hh in range(N_HEADS)]
    o_ref[0] = jnp.concatenate(parts, axis=0).T.astype(BF16)


def _causal_mask(qi, ki, tq, tk):
    s_idx = ki * tk + lax.broadcasted_iota(I32, (tk, tq), 0)
    t_idx = qi * tq + lax.broadcasted_iota(I32, (tk, tq), 1)
    return lax.shift_right_logical(s_idx, CHUNK_SHIFT) <= lax.shift_right_logical(t_idx, CHUNK_SHIFT)


def _kmax(qi, tq, tk):
    return ((qi + 1) * tq - 1) // tk


def _mla_kernel(qi_tab, ki_tab, q_ref, k_ref, vT_ref, o_ref, m_scr, l_scr, acc_scr, *, tq, tk):
    p = pl.program_id(1)
    qi = qi_tab[p]
    ki = ki_tab[p]

    @pl.when(ki == 0)
    def _():
        _flash_init(m_scr, l_scr, acc_scr)

    diag = (ki + 1) * tk > qi * tq

    def run(masked):
        mask = _causal_mask(qi, ki, tq, tk) if masked else None

        def body(h, c):
            _flash_head(h, q_ref, k_ref, vT_ref, m_scr, l_scr, acc_scr, False, mask)
            return c
        lax.fori_loop(0, N_HEADS, body, 0)

    pl.when(diag)(lambda: run(True))
    pl.when(jnp.logical_not(diag))(lambda: run(False))

    @pl.when(ki == _kmax(qi, tq, tk))
    def _():
        _flash_finalize(o_ref, l_scr, acc_scr)


def _mla_call(qa, ka, vaT, tq, tk):
    b, nh, s, _ = qa.shape
    nq = s // tq
    qi_l, ki_l = [], []
    for qi in range(nq):
        for ki in range(((qi + 1) * tq - 1) // tk + 1):
            qi_l.append(qi)
            ki_l.append(ki)
    qi_tab = jnp.asarray(np.array(qi_l, np.int32))
    ki_tab = jnp.asarray(np.array(ki_l, np.int32))
    grid_spec = pltpu.PrefetchScalarGridSpec(
        num_scalar_prefetch=2, grid=(b, len(qi_l)),
        in_specs=[
            pl.BlockSpec((1, nh, tq, LANES), lambda bi, p, qt, kt: (bi, 0, qt[p], 0)),
            pl.BlockSpec((1, nh, tk, LANES), lambda bi, p, qt, kt: (bi, 0, kt[p], 0)),
            pl.BlockSpec((1, nh * HEAD_V, tk), lambda bi, p, qt, kt: (bi, 0, kt[p])),
        ],
        out_specs=pl.BlockSpec((1, tq, nh * HEAD_V), lambda bi, p, qt, kt: (bi, qt[p], 0)),
        scratch_shapes=[pltpu.VMEM((nh, tq), F32), pltpu.VMEM((nh, tq), F32),
                        pltpu.VMEM((nh * HEAD_V, tq), F32)],
    )
    return pl.pallas_call(
        functools.partial(_mla_kernel, tq=tq, tk=tk),
        out_shape=jax.ShapeDtypeStruct((b, s, nh * HEAD_V), BF16),
        grid_spec=grid_spec,
        compiler_params=pltpu.CompilerParams(
            dimension_semantics=("arbitrary", "arbitrary"), vmem_limit_bytes=VMEM_LIMIT),
        name="mla_attn",
    )(qi_tab, ki_tab, qa, ka, vaT)


_F_LAST_A, _F_PHASE_C, _F_FIRST_C, _F_LAST_C = 1, 2, 4, 8


def _count_ge(keys_scr, nkb, trial, tq, tk):
    def body(j, acc):
        ge = (keys_scr[j] >= trial).astype(I32)
        return acc + jnp.sum(ge.reshape(tk // 8, 8, tq), axis=0)
    acc = lax.fori_loop(0, nkb, body, jnp.zeros((8, tq), I32))
    return jnp.sum(acc, axis=0, keepdims=True)


def _count_tied_before(keys_scr, nkb, thr, bound, tq, tk):
    def body(j, acc):
        idx = j * tk + lax.broadcasted_iota(I32, (tk, tq), 0)
        hit = jnp.where(keys_scr[j] == thr, (idx < bound).astype(I32), 0)
        return acc + jnp.sum(hit.reshape(tk // 8, 8, tq), axis=0)
    acc = lax.fori_loop(0, nkb, body, jnp.zeros((8, tq), I32))
    return jnp.sum(acc, axis=0, keepdims=True)


def _select_threshold(keys_scr, nkb, cand_scr, cnt_scr, thr_scr, top_k, s_len, tq, tk):
    cand_scr[...] = jnp.full((1, tq), INT_MIN, I32)
    cnt_scr[...] = jnp.full((1, tq), top_k + 1, I32)

    def cond(c):
        bit, done = c
        return jnp.logical_and(bit >= 0, done == 0)

    def step(c):
        bit, _ = c
        cand = cand_scr[...]
        trial = cand + lax.shift_left(jnp.int32(1), bit)
        cnt = _count_ge(keys_scr, nkb, trial, tq, tk)
        ok = cnt >= top_k
        cand_scr[...] = jnp.where(ok, trial, cand)
        new_cnt = jnp.where(ok, cnt, cnt_scr[...])
        cnt_scr[...] = new_cnt
        done = jnp.min(jnp.where(new_cnt == top_k, 1, 0))
        return bit - 1, done

    lax.while_loop(cond, step, (jnp.int32(31), jnp.int32(0)))

    cand = cand_scr[...]
    c_ge = cnt_scr[...]
    inexact = jnp.logical_and(c_ge != top_k, cand > INT_MIN)

    @pl.when(jnp.max(inexact.astype(I32)) > 0)
    def _():
        c_gt = _count_ge(keys_scr, nkb, cand + 1, tq, tk)
        need = top_k - c_gt
        cnt_scr[...] = jnp.zeros((1, tq), I32)
        nbits = max(1, int(math.ceil(math.log2(s_len))))

        def jstep(i, c):
            bit = nbits - 1 - i
            j0 = cnt_scr[...]
            trial = j0 + lax.shift_left(jnp.int32(1), bit)
            below = _count_tied_before(keys_scr, nkb, cand, trial, tq, tk)
            cnt_scr[...] = jnp.where(below < need, trial, j0)
            return c
        lax.fori_loop(0, nbits, jstep, 0)
        j0 = jnp.where(inexact, cnt_scr[...], s_len)

        def demote(j, c):
            idx = j * tk + lax.broadcasted_iota(I32, (tk, tq), 0)
            kj = keys_scr[j]
            keys_scr[j] = jnp.where(jnp.logical_and(kj == cand, idx > j0), kj - 1, kj)
            return c
        lax.fori_loop(0, nkb, demote, 0)

    thr_scr[...] = jnp.maximum(cand, INT_MIN + 1)


def _dsa_kernel(qi_tab, ki_tab, kia_tab, kic_tab, fl_tab,
                qi_ref, ki_ref, wT_ref, qb_ref, kb_ref, vbT_ref, o_ref,
                keys_scr, cand_scr, cnt_scr, thr_scr, m_scr, l_scr, acc_scr,
                *, tq, tk, top_k, s_len):
    p = pl.program_id(1)
    qi = qi_tab[p]
    ki = ki_tab[p]
    fl = fl_tab[p]

    @pl.when((fl & _F_PHASE_C) == 0)
    def _():
        kidx = ki_ref[0]

        def body(h, acc):
            j = lax.shift_right_logical(h, 1)
            q = qi_ref[0, j]
            lane = lax.broadcasted_iota(I32, q.shape, 1)
            q = jnp.where(lax.shift_right_logical(lane, 6) == (h & 1), q, jnp.zeros_like(q))
            lg = lax.dot_general(kidx, q, (((1,), (1,)), ((), ())), preferred_element_type=F32)
            return acc + wT_ref[0, pl.ds(h, 1), :] * jnp.maximum(lg, 0.0)
        score = lax.fori_loop(0, IDX_HEADS, body, jnp.zeros((tk, tq), F32)) + 0.0
        bits = lax.bitcast_convert_type(score, I32)
        key = bits ^ (lax.shift_right_arithmetic(bits, 31) & 0x7FFFFFFF)
        keys_scr[ki] = jnp.where(_causal_mask(qi, ki, tq, tk), key, INT_MIN)

        @pl.when((fl & _F_LAST_A) != 0)
        def _():
            _select_threshold(keys_scr, ki + 1, cand_scr, cnt_scr, thr_scr, top_k, s_len, tq, tk)

    @pl.when((fl & _F_PHASE_C) != 0)
    def _():
        @pl.when((fl & _F_FIRST_C) != 0)
        def _():
            _flash_init(m_scr, l_scr, acc_scr)

        mask = keys_scr[ki] >= thr_scr[...]

        def body(h, c):
            _flash_head(h, qb_ref, kb_ref, vbT_ref, m_scr, l_scr, acc_scr, True, mask)
            return c
        lax.fori_loop(0, N_HEADS, body, 0)

        @pl.when((fl & _F_LAST_C) != 0)
        def _():
            _flash_finalize(o_ref, l_scr, acc_scr)


def _dsa_call(qi_arr, ki_arr, wT, qb, kb, vbT, tq, tk, top_k):
    b, hp, s, _ = qb.shape
    nq, nkb = s // tq, s // tk
    qi_l, ki_l, kia_l, kic_l, fl_l = [], [], [], [], []
    for qi in range(nq):
        kmax = ((qi + 1) * tq - 1) // tk
        for ki in range(kmax + 1):
            qi_l.append(qi); ki_l.append(ki); kia_l.append(ki); kic_l.append(0)
            fl_l.append(_F_LAST_A if ki == kmax else 0)
        for ki in range(kmax + 1):
            qi_l.append(qi); ki_l.append(ki); kia_l.append(kmax); kic_l.append(ki)
            fl_l.append(_F_PHASE_C | (_F_FIRST_C if ki == 0 else 0) | (_F_LAST_C if ki == kmax else 0))
    tabs = [jnp.asarray(np.array(t, np.int32)) for t in (qi_l, ki_l, kia_l, kic_l, fl_l)]
    nh = N_HEADS
    grid_spec = pltpu.PrefetchScalarGridSpec(
        num_scalar_prefetch=5, grid=(b, len(qi_l)),
        in_specs=[
            pl.BlockSpec((1, hp, tq, LANES), lambda bi, p, qt, kt, ka, kc, f: (bi, 0, qt[p], 0)),
            pl.BlockSpec((1, tk, LANES), lambda bi, p, qt, kt, ka, kc, f: (bi, ka[p], 0)),
            pl.BlockSpec((1, IDX_HEADS, tq), lambda bi, p, qt, kt, ka, kc, f: (bi, 0, qt[p])),
            pl.BlockSpec((1, hp, tq, LANES), lambda bi, p, qt, kt, ka, kc, f: (bi, 0, qt[p], 0)),
            pl.BlockSpec((1, hp, tk, LANES), lambda bi, p, qt, kt, ka, kc, f: (bi, 0, kc[p], 0)),
            pl.BlockSpec((1, nh * HEAD_V, tk), lambda bi, p, qt, kt, ka, kc, f: (bi, 0, kc[p])),
        ],
        out_specs=pl.BlockSpec((1, tq, nh * HEAD_V), lambda bi, p, qt, kt, ka, kc, f: (bi, qt[p], 0)),
        scratch_shapes=[
            pltpu.VMEM((nkb, tk, tq), I32),
            pltpu.VMEM((1, tq), I32), pltpu.VMEM((1, tq), I32), pltpu.VMEM((1, tq), I32),
            pltpu.VMEM((nh, tq), F32), pltpu.VMEM((nh, tq), F32), pltpu.VMEM((nh * HEAD_V, tq), F32),
        ],
    )
    return pl.pallas_call(
        functools.partial(_dsa_kernel, tq=tq, tk=tk, top_k=top_k, s_len=s),
        out_shape=jax.ShapeDtypeStruct((b, s, nh * HEAD_V), BF16),
        grid_spec=grid_spec,
        compiler_params=pltpu.CompilerParams(
            dimension_semantics=("arbitrary", "arbitrary"), vmem_limit_bytes=VMEM_LIMIT),
        name="dsa_attn",
    )(*tabs, qi_arr, ki_arr, wT, qb, kb, vbT)


_R_GROUP0, _R_EXP0 = 0, N_GROUPS


def _merge_kernel(x_ref, g_ref, wg_ref, oa_ref, ob_ref, woa_ref, wob_ref, wout_ref, gf_ref,
                  wr_ref, br_ref, x1_ref, h2_ref, comb_ref):
    x = x_ref[0]
    h = _rms(x, g_ref[...]).astype(BF16)
    gates = jnp.dot(h, wg_ref[...], preferred_element_type=F32)
    ya = jnp.dot(oa_ref[0], woa_ref[...], preferred_element_type=F32)
    yb = jnp.dot(ob_ref[0], wob_ref[...], preferred_element_type=F32)
    y = jax.nn.sigmoid(gates[:, :D_MODEL]) * ya + jax.nn.sigmoid(gates[:, D_MODEL:]) * yb
    x1 = x + jnp.dot(y.astype(BF16), wout_ref[...], preferred_element_type=F32)
    x1_ref[0] = x1
    h2 = _rms(x1, gf_ref[...])
    h2_ref[0] = h2.astype(BF16)

    logits = jnp.dot(h2, wr_ref[...], preferred_element_type=F32,
                     precision=lax.Precision.HIGHEST) + br_ref[...]
    lane = lax.broadcasted_iota(I32, logits.shape, 1)
    ninf = -jnp.inf
    is_g = lane < N_GROUPS
    gl = jnp.where(is_g, logits, ninf)
    gmax = jnp.max(gl, axis=-1, keepdims=True)
    g_sel = jnp.min(jnp.where(gl == gmax, lane, LANES), axis=-1, keepdims=True)
    w_grp = 1.0 / jnp.sum(jnp.where(is_g, jnp.exp(gl - gmax), 0.0), axis=-1, keepdims=True)
    e_lane = lane - _R_EXP0
    in_grp = (e_lane >= 0) & (e_lane < N_EXPERTS) & (lax.shift_right_arithmetic(e_lane, 2) == g_sel)
    el = jnp.where(in_grp, logits, ninf)
    v1 = jnp.max(el, axis=-1, keepdims=True)
    i1 = jnp.min(jnp.where(el == v1, lane, LANES), axis=-1, keepdims=True)
    el2 = jnp.where(lane == i1, ninf, el)
    v2 = jnp.max(el2, axis=-1, keepdims=True)
    i2 = jnp.min(jnp.where(el2 == v2, lane, LANES), axis=-1, keepdims=True)
    e2 = jnp.exp(v2 - v1)
    w1 = 1.0 / (1.0 + e2)
    w2 = e2 * w1
    comb_ref[0] = jnp.where(lane == i1, w1 * w_grp, 0.0) + jnp.where(lane == i2, w2 * w_grp, 0.0)


def _merge_call(x, g_mix, wg, oa, ob, woa, wob, wout, g_ffn, wr, br, tm):
    b, s, d = x.shape
    full = lambda a: pl.BlockSpec(a.shape, lambda bi, i: (0,) * a.ndim)
    tok = lambda w: pl.BlockSpec((1, tm, w), lambda bi, i: (bi, i, 0))
    return pl.pallas_call(
        _merge_kernel,
        out_shape=(jax.ShapeDtypeStruct((b, s, d), F32), jax.ShapeDtypeStruct((b, s, d), BF16),
                   jax.ShapeDtypeStruct((b, s, LANES), F32)),
        grid=(b, s // tm),
        in_specs=[tok(d), full(g_mix), full(wg), tok(N_HEADS * HEAD_V), tok(N_HEADS * HEAD_V),
                  full(woa), full(wob), full(wout), full(g_ffn), full(wr), full(br)],
        out_specs=(tok(d), tok(d), tok(LANES)),
        compiler_params=pltpu.CompilerParams(
            dimension_semantics=("arbitrary", "arbitrary"), vmem_limit_bytes=VMEM_LIMIT),
        name="merge_router",
    )(x, g_mix, wg, oa, ob, woa, wob, wout, g_ffn, wr, br)


def _moe_kernel(h2_ref, comb_ref, x1_ref, wgate_ref, wup_ref, wdown_ref, gfin_ref, o_ref, acc_scr):
    g = pl.program_id(1)

    @pl.when(g == 0)
    def _():
        acc_scr[...] = jnp.zeros(acc_scr.shape, F32)

    h = h2_ref[...]
    comb = comb_ref[...]
    lane = lax.broadcasted_iota(I32, comb.shape, 1)
    hids = []
    for e in range(EXPERTS_PER_GROUP):
        a = jnp.dot(h, wgate_ref[e], preferred_element_type=F32)
        u = jnp.dot(h, wup_ref[e], preferred_element_type=F32)
        c = jnp.sum(jnp.where(lane == _R_EXP0 + g * EXPERTS_PER_GROUP + e, comb, 0.0),
                    axis=-1, keepdims=True)
        hids.append((a * jax.nn.sigmoid(a) * u * c).astype(BF16))
    hid = jnp.concatenate(hids, axis=1)
    wd = wdown_ref[...].reshape(EXPERTS_PER_GROUP * EXPERT_FF, D_MODEL)
    acc_scr[...] += jnp.dot(hid, wd, preferred_element_type=F32)

    @pl.when(g == N_GROUPS - 1)
    def _():
        o_ref[...] = _rms(x1_ref[...] + acc_scr[...], gfin_ref[...])


def _moe_call(h2, comb, x1, wgate, wup, wdown, g_fin, tm):
    t, d = h2.shape
    epg = EXPERTS_PER_GROUP
    return pl.pallas_call(
        _moe_kernel,
        out_shape=jax.ShapeDtypeStruct((t, d), F32),
        grid=(t // tm, N_GROUPS),
        in_specs=[
            pl.BlockSpec((tm, d), lambda i, g: (i, 0)),
            pl.BlockSpec((tm, LANES), lambda i, g: (i, 0)),
            pl.BlockSpec((tm, d), lambda i, g: (i, 0)),
            pl.BlockSpec((epg, d, EXPERT_FF), lambda i, g: (g, 0, 0)),
            pl.BlockSpec((epg, d, EXPERT_FF), lambda i, g: (g, 0, 0)),
            pl.BlockSpec((epg, EXPERT_FF, d), lambda i, g: (g, 0, 0)),
            pl.BlockSpec((1, d), lambda i, g: (0, 0)),
        ],
        out_specs=pl.BlockSpec((tm, d), lambda i, g: (i, 0)),
        scratch_shapes=[pltpu.VMEM((tm, d), F32)],
        compiler_params=pltpu.CompilerParams(
            dimension_semantics=("arbitrary", "arbitrary"), vmem_limit_bytes=VMEM_LIMIT),
        name="moe_final",
    )(h2, comb, x1, wgate, wup, wdown, g_fin)


def _rope_tables(s):
    pos = jnp.arange(s, dtype=F32)[:, None]
    lane = np.arange(LANES)

    def build(rot, lane_in_rot, first, second):
        half = rot // 2
        inv_freq = ROPE_THETA ** (-jnp.arange(half, dtype=F32) * 2.0 / rot)
        ang = pos * inv_freq[None, :]
        cos_h, sin_h = jnp.cos(ang), jnp.sin(ang)
        idx = np.where(first | second, lane_in_rot % half, 0)
        cos = jnp.where(first | second, cos_h[:, idx], 1.0)
        sin_m = jnp.where(first, -sin_h[:, idx], 0.0)
        sin_p = jnp.where(second, sin_h[:, idx], 0.0)
        return [cos, sin_m, sin_p]

    l64 = lane % 64
    a = build(DSA_ROT, l64, l64 < DSA_ROT // 2, (l64 >= DSA_ROT // 2) & (l64 < DSA_ROT))
    lb = lane - _MISC_KR
    bt = build(MLA_ROPE, np.maximum(lb, 0), (lb >= 0) & (lb < MLA_ROPE // 2),
               (lb >= MLA_ROPE // 2) & (lb < MLA_ROPE))
    return jnp.stack(a + bt).astype(F32)


def _pad_heads(w, n_heads, width):
    k = w.shape[0]
    w = w.reshape(k, n_heads, width)
    return jnp.pad(w, ((0, 0), (0, 0), (0, LANES - width))).reshape(k, n_heads * LANES)


def _tile_sizes(s):
    tm = min(256, s)
    tq = min(512, s)
    tk = min(512, s)
    return tm, tq, tk


def kernel(x, norm_mix_g, w_in, mla_q_norm_g, w_uq, mla_kv_norm_g, w_uk, w_uv, w_o_a, w_o_b, w_out,
           norm_ffn_g, w_router_group, b_router_group, w_router_expert, b_router_expert,
           w_gate, w_up, w_down, final_norm_g):
    b, s, d = x.shape
    assert d == D_MODEL and s % CHUNK == 0
    tm, tq, tk = _tile_sizes(s)
    assert s % tm == 0 and s % tq == 0 and s % tk == 0 and tq % CHUNK == 0 and tk >= DSA_TOPK_MAX
    top_k = min(DSA_TOPK_MAX, s // 4)
    tab = _rope_tables(s)
    depth = w_in.shape[0]
    assert depth == 1, "the final norm is fused into the MoE kernel of the single layer"
    row = lambda v: v.reshape(1, -1).astype(F32)

    for l in range(depth):
        sizes = (MLA_Q_LORA, MLA_KV_LORA, MLA_ROPE, 512, 512, 512, 512, IDX_DIM, IDX_HEADS, D_MODEL, D_MODEL)
        offs = np.concatenate([[0], np.cumsum(sizes)])
        seg = [w_in[l][:, offs[i]:offs[i + 1]] for i in range(len(sizes))]
        (w_cq, w_ckv, w_kr, w_qb, w_kb, w_vb, w_qi, w_ki, w_wi, w_ga, w_gb) = seg
        misc = jnp.concatenate([jnp.zeros((d, _MISC_KR), F32), w_kr, w_wi,
                                jnp.zeros((d, LANES - _MISC_W - IDX_HEADS), F32)], axis=1)
        w1 = jnp.concatenate([w_cq, w_ckv, w_qb, w_kb, w_vb, w_qi, w_ki, w_ki, misc], axis=1).astype(BF16)
        wg = jnp.concatenate([w_ga, w_gb], axis=1).astype(BF16)
        wuq = _pad_heads(w_uq[l], N_HEADS, MLA_NOPE + MLA_ROPE).astype(BF16)
        wuk = _pad_heads(w_uk[l], N_HEADS, MLA_NOPE).astype(BF16)
        wuv = w_uv[l].astype(BF16)

        qa, ka, vaT, qb, kb, vbT, qi, ki, wT = _inproj_call(
            x, row(norm_mix_g[l]), w1, wuq, wuk, wuv, row(mla_q_norm_g[l]), row(mla_kv_norm_g[l]), tab, tm)
        o_a = _mla_call(qa, ka, vaT, tq, tk)
        o_b = _dsa_call(qi, ki, wT, qb, kb, vbT, tq, tk, top_k)

        wr = jnp.concatenate([w_router_group[l], w_router_expert[l],
                              jnp.zeros((d, LANES - N_GROUPS - N_EXPERTS), F32)], axis=1)
        br = jnp.concatenate([b_router_group[l], b_router_expert[l],
                              jnp.zeros((LANES - N_GROUPS - N_EXPERTS,), F32)]).reshape(1, LANES)
        x1, h2, comb = _merge_call(x, row(norm_mix_g[l]), wg, o_a, o_b, w_o_a[l].astype(BF16),
                                   w_o_b[l].astype(BF16), w_out[l].astype(BF16), row(norm_ffn_g[l]), wr, br, tm)
        tm_moe = min(1024, b * s)
        y = _moe_call(h2.reshape(b * s, d), comb.reshape(b * s, LANES), x1.reshape(b * s, d),
                      w_gate[l].astype(BF16), w_up[l].astype(BF16), w_down[l].astype(BF16),
                      row(final_norm_g), tm_moe)
        x = y.reshape(b, s, d)
    return x
```

```python
import functools
import math

import numpy as np
import jax
import jax.numpy as jnp
from jax import lax
from jax.experimental import pallas as pl
from jax.experimental.pallas import tpu as pltpu

F32 = jnp.float32
BF16 = jnp.bfloat16
I32 = jnp.int32

D_MODEL = 1024
CHUNK = 64
CHUNK_SHIFT = 6
ROPE_THETA = 500000.0
EPS = 1e-6
NEG = -1e30
LOG2E = 1.4426950408889634

N_HEADS = 8
HEAD_V = 64
HEAD_VA = 80
MLA_Q_LORA = 384
MLA_KV_LORA = 256
MLA_NOPE = 64
MLA_ROPE = 32
DSA_HEAD_DIM = 64
DSA_ROT = 16
IDX_HEADS = 8
IDX_DIM = 64
DSA_TOPK_MAX = 256

N_GROUPS = 8
EXPERTS_PER_GROUP = 4
N_EXPERTS = N_GROUPS * EXPERTS_PER_GROUP
EXPERT_FF = 256

LANES = 128
INT_MIN = -(2 ** 31)
HEAD_UNROLL = 2
VMEM_LIMIT = 52 * 1024 * 1024

_C_CQ, _C_CKV, _C_QB, _C_KB, _C_VB, _C_QI, _C_KI, _C_MISC, _C_END = (
    0, 384, 640, 1152, 1664, 2176, 2688, 2816, 2944)
_MISC_KR = 64
_MISC_W = 96


def _rms(x, g):
    return x * lax.rsqrt(jnp.mean(x * x, axis=-1, keepdims=True) + EPS) * g


def _tile_lanes(t, width):
    reps = width // LANES
    return t if reps == 1 else jnp.concatenate([t] * reps, axis=1)


def _rope(x, cos, sin_m, sin_p, half):
    w = x.shape[1]
    c, sm, sp = (_tile_lanes(t, w) for t in (cos, sin_m, sin_p))
    return x * c + pltpu.roll(x, w - half, 1) * sm + pltpu.roll(x, half, 1) * sp


def _aug_transpose(v):
    vt = v.T
    ones = jnp.ones((HEAD_VA - HEAD_V, vt.shape[1]), F32)
    parts = []
    for hh in range(N_HEADS):
        parts += [vt[hh * HEAD_V:(hh + 1) * HEAD_V, :], ones]
    return jnp.concatenate(parts, axis=0).astype(BF16)


def _inproj_kernel(x_ref, g_ref, w1_ref, wuq_ref, wuk_ref, wuv_ref, gq_ref, gkv_ref, tab_ref,
                   qa_ref, ka_ref, vaT_ref, qb_ref, kb_ref, vbT_ref, qi_ref, ki_ref, wT_ref,
                   *, scale_a):
    x = x_ref[0]
    h = _rms(x, g_ref[...]).astype(BF16)

    def proj(a, b):
        return jnp.dot(h, w1_ref[:, a:b], preferred_element_type=F32)

    cos_a, sin_am, sin_ap = tab_ref[0], tab_ref[1], tab_ref[2]
    cos_b, sin_bm, sin_bp = tab_ref[3], tab_ref[4], tab_ref[5]

    cq = _rms(proj(_C_CQ, _C_CKV), gq_ref[...]).astype(BF16)
    qa = jnp.dot(cq, wuq_ref[...], preferred_element_type=F32)
    qa = _rope(qa, cos_b, sin_bm, sin_bp, MLA_ROPE // 2) * scale_a
    ckv = _rms(proj(_C_CKV, _C_QB), gkv_ref[...]).astype(BF16)
    kn = jnp.dot(ckv, wuk_ref[...], preferred_element_type=F32)
    misc = _rope(proj(_C_MISC, _C_END), cos_b, sin_bm, sin_bp, MLA_ROPE // 2)
    lane = lax.broadcasted_iota(I32, misc.shape, 1)
    kpe = jnp.where((lane >= _MISC_KR) & (lane < _MISC_KR + MLA_ROPE), misc, 0.0)
    for hh in range(N_HEADS):
        sl = slice(hh * LANES, (hh + 1) * LANES)
        qa_ref[0, hh] = qa[:, sl].astype(BF16)
        ka_ref[0, hh] = (kn[:, sl] + kpe).astype(BF16)
    va = jnp.dot(ckv, wuv_ref[...], preferred_element_type=F32)
    vaT_ref[0] = _aug_transpose(va)

    qb = _rope(proj(_C_QB, _C_KB), cos_a, sin_am, sin_ap, DSA_ROT // 2) * (DSA_HEAD_DIM ** -0.5 * LOG2E)
    kb = _rope(proj(_C_KB, _C_VB), cos_a, sin_am, sin_ap, DSA_ROT // 2)
    qi = _rope(proj(_C_QI, _C_KI), cos_a, sin_am, sin_ap, DSA_ROT // 2) * (IDX_DIM ** -0.5)
    for j in range(N_HEADS // 2):
        sl = slice(j * LANES, (j + 1) * LANES)
        qb_ref[0, j] = qb[:, sl].astype(BF16)
        kb_ref[0, j] = kb[:, sl].astype(BF16)
        qi_ref[0, j] = qi[:, sl].astype(BF16)
    vbT_ref[0] = _aug_transpose(proj(_C_VB, _C_QI))
    ki2 = _rope(proj(_C_KI, _C_MISC), cos_a, sin_am, sin_ap, DSA_ROT // 2)
    ki_ref[0] = ki2.astype(BF16)
    wT_ref[0] = misc.T[_MISC_W:_MISC_W + IDX_HEADS, :] * (IDX_HEADS ** -0.5)


def _inproj_call(x, g_mix, w1, wuq, wuk, wuv, gq, gkv, tab, tm):
    b, s, d = x.shape
    nh, hp = N_HEADS, N_HEADS // 2
    full = lambda shape: pl.BlockSpec(shape, lambda bi, i: (0,) * len(shape))
    out_shape = (
        jax.ShapeDtypeStruct((b, nh, s, LANES), BF16),
        jax.ShapeDtypeStruct((b, nh, s, LANES), BF16),
        jax.ShapeDtypeStruct((b, nh * HEAD_VA, s), BF16),
        jax.ShapeDtypeStruct((b, hp, s, LANES), BF16),
        jax.ShapeDtypeStruct((b, hp, s, LANES), BF16),
        jax.ShapeDtypeStruct((b, nh * HEAD_VA, s), BF16),
        jax.ShapeDtypeStruct((b, hp, s, LANES), BF16),
        jax.ShapeDtypeStruct((b, s, LANES), BF16),
        jax.ShapeDtypeStruct((b, IDX_HEADS, s), F32),
    )
    head_spec = lambda n: pl.BlockSpec((1, n, tm, LANES), lambda bi, i: (bi, 0, i, 0))
    t_spec = pl.BlockSpec((1, nh * HEAD_VA, tm), lambda bi, i: (bi, 0, i))
    out_specs = (head_spec(nh), head_spec(nh), t_spec, head_spec(hp), head_spec(hp), t_spec,
                 head_spec(hp), pl.BlockSpec((1, tm, LANES), lambda bi, i: (bi, i, 0)),
                 pl.BlockSpec((1, IDX_HEADS, tm), lambda bi, i: (bi, 0, i)))
    in_specs = [
        pl.BlockSpec((1, tm, d), lambda bi, i: (bi, i, 0)),
        full(g_mix.shape), full(w1.shape), full(wuq.shape), full(wuk.shape), full(wuv.shape),
        full(gq.shape), full(gkv.shape),
        pl.BlockSpec((tab.shape[0], tm, LANES), lambda bi, i: (0, i, 0)),
    ]
    scale_a = (MLA_NOPE + MLA_ROPE) ** -0.5 * LOG2E
    return pl.pallas_call(
        functools.partial(_inproj_kernel, scale_a=scale_a),
        out_shape=out_shape, grid=(b, s // tm), in_specs=in_specs, out_specs=out_specs,
        compiler_params=pltpu.CompilerParams(
            dimension_semantics=("arbitrary", "arbitrary"), vmem_limit_bytes=VMEM_LIMIT),
        name="inproj",
    )(x, g_mix, w1, wuq, wuk, wuv, gq, gkv, tab)


def _head_scores(h, q_ref, k_ref, pairs):
    if pairs:
        q = q_ref[0, h // 2]
        lane = lax.broadcasted_iota(I32, q.shape, 1)
        q = jnp.where(lax.shift_right_logical(lane, 6) == (h % 2), q, jnp.zeros_like(q))
        k = k_ref[0, h // 2]
    else:
        q = q_ref[0, h]
        k = k_ref[0, h]
    return lax.dot_general(k, q, (((1,), (1,)), ((), ())), preferred_element_type=F32)


def _flash_heads(q_ref, k_ref, vT_ref, m_scr, acc_scr, pairs, mask):
    s_next = _head_scores(0, q_ref, k_ref, pairs)
    for h in range(N_HEADS):
        s = s_next
        if h + 1 < N_HEADS:
            s_next = _head_scores(h + 1, q_ref, k_ref, pairs)
        _flash_update(h, s, vT_ref, m_scr, acc_scr, mask)


def _flash_update(h, s, vT_ref, m_scr, acc_scr, mask):
    if mask is not None:
        s = jnp.where(mask, s, NEG)
    row = slice(h, h + 1)
    m_prev = m_scr[row, :]
    m_new = jnp.maximum(m_prev, jnp.max(s, axis=0, keepdims=True))
    p = jnp.exp2(s - m_new).astype(BF16)
    alpha = jnp.exp2(m_prev - m_new)
    m_scr[row, :] = m_new
    rows = slice(h * HEAD_VA, (h + 1) * HEAD_VA)
    pv = jnp.dot(vT_ref[0, rows, :], p, preferred_element_type=F32)
    acc_scr[rows, :] = alpha * acc_scr[rows, :] + pv


def _flash_init(m_scr, acc_scr):
    m_scr[...] = jnp.full(m_scr.shape, NEG, F32)
    acc_scr[...] = jnp.zeros(acc_scr.shape, F32)


def _flash_finalize(o_ref, acc_scr):
    parts = []
    for hh in range(N_HEADS):
        r0 = hh * HEAD_VA
        parts.append(acc_scr[r0:r0 + HEAD_V, :] * (1.0 / acc_scr[r0 + HEAD_V:r0 + HEAD_V + 1, :]))
    o_ref[0] = jnp.concatenate(parts, axis=0).T.astype(BF16)


def _causal_mask(qi, ki, tq, tk):
    s_idx = ki * tk + lax.broadcasted_iota(I32, (tk, tq), 0)
    t_idx = qi * tq + lax.broadcasted_iota(I32, (tk, tq), 1)
    return lax.shift_right_logical(s_idx, CHUNK_SHIFT) <= lax.shift_right_logical(t_idx, CHUNK_SHIFT)


def _kmax(qi, tq, tk):
    return ((qi + 1) * tq - 1) // tk


def _mla_kernel(qi_tab, ki_tab, q_ref, k_ref, vT_ref, o_ref, m_scr, acc_scr, *, tq, tk):
    p = pl.program_id(1)
    qi = qi_tab[p]
    ki = ki_tab[p]

    @pl.when(ki == 0)
    def _():
        _flash_init(m_scr, acc_scr)

    diag = (ki + 1) * tk > qi * tq

    def run(masked):
        mask = _causal_mask(qi, ki, tq, tk) if masked else None

        _flash_heads(q_ref, k_ref, vT_ref, m_scr, acc_scr, False, mask)

    pl.when(diag)(lambda: run(True))
    pl.when(jnp.logical_not(diag))(lambda: run(False))

    @pl.when(ki == _kmax(qi, tq, tk))
    def _():
        _flash_finalize(o_ref, acc_scr)


def _mla_call(qa, ka, vaT, tq, tk):
    b, nh, s, _ = qa.shape
    nq = s // tq
    qi_l, ki_l = [], []
    for qi in range(nq):
        for ki in range(((qi + 1) * tq - 1) // tk + 1):
            qi_l.append(qi)
            ki_l.append(ki)
    qi_tab = jnp.asarray(np.array(qi_l, np.int32))
    ki_tab = jnp.asarray(np.array(ki_l, np.int32))
    grid_spec = pltpu.PrefetchScalarGridSpec(
        num_scalar_prefetch=2, grid=(b, len(qi_l)),
        in_specs=[
            pl.BlockSpec((1, nh, tq, LANES), lambda bi, p, qt, kt: (bi, 0, qt[p], 0)),
            pl.BlockSpec((1, nh, tk, LANES), lambda bi, p, qt, kt: (bi, 0, kt[p], 0)),
            pl.BlockSpec((1, nh * HEAD_VA, tk), lambda bi, p, qt, kt: (bi, 0, kt[p])),
        ],
        out_specs=pl.BlockSpec((1, tq, nh * HEAD_V), lambda bi, p, qt, kt: (bi, qt[p], 0)),
        scratch_shapes=[pltpu.VMEM((nh, tq), F32), pltpu.VMEM((nh * HEAD_VA, tq), F32)],
    )
    return pl.pallas_call(
        functools.partial(_mla_kernel, tq=tq, tk=tk),
        out_shape=jax.ShapeDtypeStruct((b, s, nh * HEAD_V), BF16),
        grid_spec=grid_spec,
        compiler_params=pltpu.CompilerParams(
            dimension_semantics=("arbitrary", "arbitrary"), vmem_limit_bytes=VMEM_LIMIT),
        name="mla_attn",
    )(qi_tab, ki_tab, qa, ka, vaT)


_F_LAST_A, _F_PHASE_C, _F_FIRST_C, _F_LAST_C = 1, 2, 4, 8


KEY_LOWEST_FINITE = -2139095040


def _key_to_f32(k):
    return lax.bitcast_convert_type(k ^ (lax.shift_right_arithmetic(k, 31) & 0x7FFFFFFF), F32)


def _count(sc_scr, nkb, trial, tq, tk, strict=False):
    def body(j, acc):
        sc = sc_scr[j]
        hit = ((sc > trial) if strict else (sc >= trial)).astype(I32)
        return acc + jnp.sum(hit.reshape(tk // 8, 8, tq), axis=0)
    acc = lax.fori_loop(0, nkb, body, jnp.zeros((8, tq), I32))
    return jnp.sum(acc, axis=0, keepdims=True)


def _count_tied_before(sc_scr, nkb, thr, bound, tq, tk):
    def body(j, acc):
        idx = j * tk + lax.broadcasted_iota(I32, (tk, tq), 0)
        hit = jnp.where(sc_scr[j] == thr, (idx < bound).astype(I32), 0)
        return acc + jnp.sum(hit.reshape(tk // 8, 8, tq), axis=0)
    acc = lax.fori_loop(0, nkb, body, jnp.zeros((8, tq), I32))
    return jnp.sum(acc, axis=0, keepdims=True)


def _select_threshold(sc_scr, nkb, cand_scr, cnt_scr, thr_scr, top_k, s_len, tq, tk):
    cand_scr[...] = jnp.full((1, tq), INT_MIN, I32)
    cnt_scr[...] = jnp.full((1, tq), top_k + 1, I32)

    def cond(c):
        bit, done = c
        return jnp.logical_and(bit >= 0, done == 0)

    def step(c):
        bit, _ = c
        cand = cand_scr[...]
        trial = cand + lax.shift_left(jnp.int32(1), bit)
        cnt = _count(sc_scr, nkb, _key_to_f32(trial), tq, tk)
        ok = cnt >= top_k
        cand_scr[...] = jnp.where(ok, trial, cand)
        new_cnt = jnp.where(ok, cnt, cnt_scr[...])
        cnt_scr[...] = new_cnt
        done = jnp.min(jnp.where(new_cnt == top_k, 1, 0))
        return bit - 1, done

    lax.while_loop(cond, step, (jnp.int32(31), jnp.int32(0)))

    cand = cand_scr[...]
    thr = _key_to_f32(jnp.maximum(cand, KEY_LOWEST_FINITE))
    inexact = jnp.logical_and(cnt_scr[...] != top_k, cand >= KEY_LOWEST_FINITE)

    @pl.when(jnp.max(inexact.astype(I32)) > 0)
    def _():
        need = top_k - _count(sc_scr, nkb, thr, tq, tk, strict=True)
        cnt_scr[...] = jnp.zeros((1, tq), I32)
        nbits = max(1, int(math.ceil(math.log2(s_len))))

        def jstep(i, c):
            bit = nbits - 1 - i
            j0 = cnt_scr[...]
            trial = j0 + lax.shift_left(jnp.int32(1), bit)
            below = _count_tied_before(sc_scr, nkb, thr, trial, tq, tk)
            cnt_scr[...] = jnp.where(below < need, trial, j0)
            return c
        lax.fori_loop(0, nbits, jstep, 0)
        j0 = jnp.where(inexact, cnt_scr[...], s_len)

        def demote(j, c):
            idx = j * tk + lax.broadcasted_iota(I32, (tk, tq), 0)
            sj = sc_scr[j]
            sc_scr[j] = jnp.where(jnp.logical_and(sj == thr, idx > j0), -jnp.inf, sj)
            return c
        lax.fori_loop(0, nkb, demote, 0)

    thr_scr[...] = thr


def _dsa_kernel(qi_tab, ki_tab, kia_tab, kic_tab, fl_tab,
                qi_ref, ki_ref, wT_ref, qb_ref, kb_ref, vbT_ref, o_ref,
                sc_scr, cand_scr, cnt_scr, thr_scr, m_scr, acc_scr,
                *, tq, tk, top_k, s_len):
    p = pl.program_id(1)
    qi = qi_tab[p]
    ki = ki_tab[p]
    fl = fl_tab[p]

    @pl.when((fl & _F_PHASE_C) == 0)
    def _():
        kidx = ki_ref[0]

        score = None
        for h in range(IDX_HEADS):
            q = qi_ref[0, h // 2]
            lane = lax.broadcasted_iota(I32, q.shape, 1)
            q = jnp.where(lax.shift_right_logical(lane, 6) == (h % 2), q, jnp.zeros_like(q))
            lg = lax.dot_general(kidx, q, (((1,), (1,)), ((), ())), preferred_element_type=F32)
            t = wT_ref[0, h:h + 1, :] * jnp.maximum(lg, 0.0)
            score = t if score is None else score + t
        sc_scr[ki] = jnp.where(_causal_mask(qi, ki, tq, tk), score, -jnp.inf)

        @pl.when((fl & _F_LAST_A) != 0)
        def _():
            _select_threshold(sc_scr, ki + 1, cand_scr, cnt_scr, thr_scr, top_k, s_len, tq, tk)

    @pl.when((fl & _F_PHASE_C) != 0)
    def _():
        @pl.when((fl & _F_FIRST_C) != 0)
        def _():
            _flash_init(m_scr, acc_scr)

        mask = sc_scr[ki] >= thr_scr[...]

        _flash_heads(qb_ref, kb_ref, vbT_ref, m_scr, acc_scr, True, mask)

        @pl.when((fl & _F_LAST_C) != 0)
        def _():
            _flash_finalize(o_ref, acc_scr)


def _dsa_call(qi_arr, ki_arr, wT, qb, kb, vbT, tq, tk, top_k):
    b, hp, s, _ = qb.shape
    nq, nkb = s // tq, s // tk
    qi_l, ki_l, kia_l, kic_l, fl_l = [], [], [], [], []
    for qi in range(nq):
        kmax = ((qi + 1) * tq - 1) // tk
        for ki in range(kmax + 1):
            qi_l.append(qi); ki_l.append(ki); kia_l.append(ki); kic_l.append(0)
            fl_l.append(_F_LAST_A if ki == kmax else 0)
        for ki in range(kmax + 1):
            qi_l.append(qi); ki_l.append(ki); kia_l.append(kmax); kic_l.append(ki)
            fl_l.append(_F_PHASE_C | (_F_FIRST_C if ki == 0 else 0) | (_F_LAST_C if ki == kmax else 0))
    tabs = [jnp.asarray(np.array(t, np.int32)) for t in (qi_l, ki_l, kia_l, kic_l, fl_l)]
    nh = N_HEADS
    grid_spec = pltpu.PrefetchScalarGridSpec(
        num_scalar_prefetch=5, grid=(b, len(qi_l)),
        in_specs=[
            pl.BlockSpec((1, hp, tq, LANES), lambda bi, p, qt, kt, ka, kc, f: (bi, 0, qt[p], 0)),
            pl.BlockSpec((1, tk, LANES), lambda bi, p, qt, kt, ka, kc, f: (bi, ka[p], 0)),
            pl.BlockSpec((1, IDX_HEADS, tq), lambda bi, p, qt, kt, ka, kc, f: (bi, 0, qt[p])),
            pl.BlockSpec((1, hp, tq, LANES), lambda bi, p, qt, kt, ka, kc, f: (bi, 0, qt[p], 0)),
            pl.BlockSpec((1, hp, tk, LANES), lambda bi, p, qt, kt, ka, kc, f: (bi, 0, kc[p], 0)),
            pl.BlockSpec((1, nh * HEAD_VA, tk), lambda bi, p, qt, kt, ka, kc, f: (bi, 0, kc[p])),
        ],
        out_specs=pl.BlockSpec((1, tq, nh * HEAD_V), lambda bi, p, qt, kt, ka, kc, f: (bi, qt[p], 0)),
        scratch_shapes=[
            pltpu.VMEM((nkb, tk, tq), F32),
            pltpu.VMEM((1, tq), I32), pltpu.VMEM((1, tq), I32), pltpu.VMEM((1, tq), F32),
            pltpu.VMEM((nh, tq), F32), pltpu.VMEM((nh * HEAD_VA, tq), F32),
        ],
    )
    return pl.pallas_call(
        functools.partial(_dsa_kernel, tq=tq, tk=tk, top_k=top_k, s_len=s),
        out_shape=jax.ShapeDtypeStruct((b, s, nh * HEAD_V), BF16),
        grid_spec=grid_spec,
        compiler_params=pltpu.CompilerParams(
            dimension_semantics=("arbitrary", "arbitrary"), vmem_limit_bytes=VMEM_LIMIT),
        name="dsa_attn",
    )(*tabs, qi_arr, ki_arr, wT, qb, kb, vbT)


_R_GROUP0, _R_EXP0 = 0, N_GROUPS


def _merge_kernel(x_ref, g_ref, wg_ref, oa_ref, ob_ref, woa_ref, wob_ref, wout_ref, gf_ref,
                  wr_ref, br_ref, x1_ref, h2_ref, comb_ref):
    x = x_ref[0]
    h = _rms(x, g_ref[...]).astype(BF16)
    gates = jnp.dot(h, wg_ref[...], preferred_element_type=F32)
    ya = jnp.dot(oa_ref[0], woa_ref[...], preferred_element_type=F32)
    yb = jnp.dot(ob_ref[0], wob_ref[...], preferred_element_type=F32)
    y = jax.nn.sigmoid(gates[:, :D_MODEL]) * ya + jax.nn.sigmoid(gates[:, D_MODEL:]) * yb
    x1 = x + jnp.dot(y.astype(BF16), wout_ref[...], preferred_element_type=F32)
    x1_ref[0] = x1
    h2 = _rms(x1, gf_ref[...])
    h2_ref[0] = h2.astype(BF16)

    logits = jnp.dot(h2, wr_ref[...], preferred_element_type=F32,
                     precision=lax.Precision.HIGHEST) + br_ref[...]
    lane = lax.broadcasted_iota(I32, logits.shape, 1)
    ninf = -jnp.inf
    is_g = lane < N_GROUPS
    gl = jnp.where(is_g, logits, ninf)
    gmax = jnp.max(gl, axis=-1, keepdims=True)
    g_sel = jnp.min(jnp.where(gl == gmax, lane, LANES), axis=-1, keepdims=True)
    w_grp = 1.0 / jnp.sum(jnp.where(is_g, jnp.exp(gl - gmax), 0.0), axis=-1, keepdims=True)
    e_lane = lane - _R_EXP0
    in_grp = (e_lane >= 0) & (e_lane < N_EXPERTS) & (lax.shift_right_arithmetic(e_lane, 2) == g_sel)
    el = jnp.where(in_grp, logits, ninf)
    v1 = jnp.max(el, axis=-1, keepdims=True)
    i1 = jnp.min(jnp.where(el == v1, lane, LANES), axis=-1, keepdims=True)
    el2 = jnp.where(lane == i1, ninf, el)
    v2 = jnp.max(el2, axis=-1, keepdims=True)
    i2 = jnp.min(jnp.where(el2 == v2, lane, LANES), axis=-1, keepdims=True)
    e2 = jnp.exp(v2 - v1)
    w1 = 1.0 / (1.0 + e2)
    w2 = e2 * w1
    comb_ref[0] = jnp.where(lane == i1, w1 * w_grp, 0.0) + jnp.where(lane == i2, w2 * w_grp, 0.0)


def _merge_call(x, g_mix, wg, oa, ob, woa, wob, wout, g_ffn, wr, br, tm):
    b, s, d = x.shape
    full = lambda a: pl.BlockSpec(a.shape, lambda bi, i: (0,) * a.ndim)
    tok = lambda w: pl.BlockSpec((1, tm, w), lambda bi, i: (bi, i, 0))
    return pl.pallas_call(
        _merge_kernel,
        out_shape=(jax.ShapeDtypeStruct((b, s, d), F32), jax.ShapeDtypeStruct((b, s, d), BF16),
                   jax.ShapeDtypeStruct((b, s, LANES), F32)),
        grid=(b, s // tm),
        in_specs=[tok(d), full(g_mix), full(wg), tok(N_HEADS * HEAD_V), tok(N_HEADS * HEAD_V),
                  full(woa), full(wob), full(wout), full(g_ffn), full(wr), full(br)],
        out_specs=(tok(d), tok(d), tok(LANES)),
        compiler_params=pltpu.CompilerParams(
            dimension_semantics=("arbitrary", "arbitrary"), vmem_limit_bytes=VMEM_LIMIT),
        name="merge_router",
    )(x, g_mix, wg, oa, ob, woa, wob, wout, g_ffn, wr, br)


def _moe_kernel(h2_ref, comb_ref, x1_ref, wgate_ref, wup_ref, wdown_ref, gfin_ref, o_ref, acc_scr):
    g = pl.program_id(1)

    @pl.when(g == 0)
    def _():
        acc_scr[...] = jnp.zeros(acc_scr.shape, F32)

    h = h2_ref[...]
    comb = comb_ref[...]
    lane = lax.broadcasted_iota(I32, comb.shape, 1)
    hids = []
    for e in range(EXPERTS_PER_GROUP):
        a = jnp.dot(h, wgate_ref[e], preferred_element_type=F32)
        u = jnp.dot(h, wup_ref[e], preferred_element_type=F32)
        c = jnp.sum(jnp.where(lane == _R_EXP0 + g * EXPERTS_PER_GROUP + e, comb, 0.0),
                    axis=-1, keepdims=True)
        hids.append((a * jax.nn.sigmoid(a) * u * c).astype(BF16))
    hid = jnp.concatenate(hids, axis=1)
    wd = wdown_ref[...].reshape(EXPERTS_PER_GROUP * EXPERT_FF, D_MODEL)
    acc_scr[...] += jnp.dot(hid, wd, preferred_element_type=F32)

    @pl.when(g == N_GROUPS - 1)
    def _():
        o_ref[...] = _rms(x1_ref[...] + acc_scr[...], gfin_ref[...])


def _moe_call(h2, comb, x1, wgate, wup, wdown, g_fin, tm):
    t, d = h2.shape
    epg = EXPERTS_PER_GROUP
    return pl.pallas_call(
        _moe_kernel,
        out_shape=jax.ShapeDtypeStruct((t, d), F32),
        grid=(t // tm, N_GROUPS),
        in_specs=[
            pl.BlockSpec((tm, d), lambda i, g: (i, 0)),
            pl.BlockSpec((tm, LANES), lambda i, g: (i, 0)),
            pl.BlockSpec((tm, d), lambda i, g: (i, 0)),
            pl.BlockSpec((epg, d, EXPERT_FF), lambda i, g: (g, 0, 0)),
            pl.BlockSpec((epg, d, EXPERT_FF), lambda i, g: (g, 0, 0)),
            pl.BlockSpec((epg, EXPERT_FF, d), lambda i, g: (g, 0, 0)),
            pl.BlockSpec((1, d), lambda i, g: (0, 0)),
        ],
        out_specs=pl.BlockSpec((tm, d), lambda i, g: (i, 0)),
        scratch_shapes=[pltpu.VMEM((tm, d), F32)],
        compiler_params=pltpu.CompilerParams(
            dimension_semantics=("arbitrary", "arbitrary"), vmem_limit_bytes=VMEM_LIMIT),
        name="moe_final",
    )(h2, comb, x1, wgate, wup, wdown, g_fin)


def _rope_tables(s):
    pos = jnp.arange(s, dtype=F32)[:, None]
    lane = np.arange(LANES)

    def build(rot, lane_in_rot, first, second):
        half = rot // 2
        inv_freq = ROPE_THETA ** (-jnp.arange(half, dtype=F32) * 2.0 / rot)
        ang = pos * inv_freq[None, :]
        cos_h, sin_h = jnp.cos(ang), jnp.sin(ang)
        idx = np.where(first | second, lane_in_rot % half, 0)
        cos = jnp.where(first | second, cos_h[:, idx], 1.0)
        sin_m = jnp.where(first, -sin_h[:, idx], 0.0)
        sin_p = jnp.where(second, sin_h[:, idx], 0.0)
        return [cos, sin_m, sin_p]

    l64 = lane % 64
    a = build(DSA_ROT, l64, l64 < DSA_ROT // 2, (l64 >= DSA_ROT // 2) & (l64 < DSA_ROT))
    lb = lane - _MISC_KR
    bt = build(MLA_ROPE, np.maximum(lb, 0), (lb >= 0) & (lb < MLA_ROPE // 2),
               (lb >= MLA_ROPE // 2) & (lb < MLA_ROPE))
    return jnp.stack(a + bt).astype(F32)


def _pad_heads(w, n_heads, width):
    k = w.shape[0]
    w = w.reshape(k, n_heads, width)
    return jnp.pad(w, ((0, 0), (0, 0), (0, LANES - width))).reshape(k, n_heads * LANES)


def _tile_sizes(s):
    tm = min(256, s)
    tq = min(512, s)
    tk = min(512, s)
    return tm, tq, tk


def kernel(x, norm_mix_g, w_in, mla_q_norm_g, w_uq, mla_kv_norm_g, w_uk, w_uv, w_o_a, w_o_b, w_out,
           norm_ffn_g, w_router_group, b_router_group, w_router_expert, b_router_expert,
           w_gate, w_up, w_down, final_norm_g):
    b, s, d = x.shape
    assert d == D_MODEL and s % CHUNK == 0
    tm, tq, tk = _tile_sizes(s)
    assert s % tm == 0 and s % tq == 0 and s % tk == 0 and tq % CHUNK == 0 and tk >= DSA_TOPK_MAX
    top_k = min(DSA_TOPK_MAX, s // 4)
    tab = _rope_tables(s)
    depth = w_in.shape[0]
    assert depth == 1, "the final norm is fused into the MoE kernel of the single layer"
    row = lambda v: v.reshape(1, -1).astype(F32)

    for l in range(depth):
        sizes = (MLA_Q_LORA, MLA_KV_LORA, MLA_ROPE, 512, 512, 512, 512, IDX_DIM, IDX_HEADS, D_MODEL, D_MODEL)
        offs = np.concatenate([[0], np.cumsum(sizes)])
        seg = [w_in[l][:, offs[i]:offs[i + 1]] for i in range(len(sizes))]
        (w_cq, w_ckv, w_kr, w_qb, w_kb, w_vb, w_qi, w_ki, w_wi, w_ga, w_gb) = seg
        misc = jnp.concatenate([jnp.zeros((d, _MISC_KR), F32), w_kr, w_wi,
                                jnp.zeros((d, LANES - _MISC_W - IDX_HEADS), F32)], axis=1)
        w1 = jnp.concatenate([w_cq, w_ckv, w_qb, w_kb, w_vb, w_qi, w_ki, w_ki, misc], axis=1).astype(BF16)
        wg = jnp.concatenate([w_ga, w_gb], axis=1).astype(BF16)
        wuq = _pad_heads(w_uq[l], N_HEADS, MLA_NOPE + MLA_ROPE).astype(BF16)
        wuk = _pad_heads(w_uk[l], N_HEADS, MLA_NOPE).astype(BF16)
        wuv = w_uv[l].astype(BF16)

        qa, ka, vaT, qb, kb, vbT, qi, ki, wT = _inproj_call(
            x, row(norm_mix_g[l]), w1, wuq, wuk, wuv, row(mla_q_norm_g[l]), row(mla_kv_norm_g[l]), tab, tm)
        o_a = _mla_call(qa, ka, vaT, tq, tk)
        o_b = _dsa_call(qi, ki, wT, qb, kb, vbT, tq, tk, top_k)

        wr = jnp.concatenate([w_router_group[l], w_router_expert[l],
                              jnp.zeros((d, LANES - N_GROUPS - N_EXPERTS), F32)], axis=1)
        br = jnp.concatenate([b_router_group[l], b_router_expert[l],
                              jnp.zeros((LANES - N_GROUPS - N_EXPERTS,), F32)]).reshape(1, LANES)
        x1, h2, comb = _merge_call(x, row(norm_mix_g[l]), wg, o_a, o_b, w_o_a[l].astype(BF16),
                                   w_o_b[l].astype(BF16), w_out[l].astype(BF16), row(norm_ffn_g[l]), wr, br, tm)
        tm_moe = min(1024, b * s)
        y = _moe_call(h2.reshape(b * s, d), comb.reshape(b * s, LANES), x1.reshape(b * s, d),
                      w_gate[l].astype(BF16), w_up[l].astype(BF16), w_down[l].astype(BF16),
                      row(final_norm_g), tm_moe)
        x = y.reshape(b, s, d)
    return x
```

```python
import functools
import math

import numpy as np
import jax
import jax.numpy as jnp
from jax import lax
from jax.experimental import pallas as pl
from jax.experimental.pallas import tpu as pltpu

F32 = jnp.float32
BF16 = jnp.bfloat16
I32 = jnp.int32

D_MODEL = 1024
CHUNK = 64
CHUNK_SHIFT = 6
ROPE_THETA = 500000.0
EPS = 1e-6
NEG = -1e30
LOG2E = 1.4426950408889634

N_HEADS = 8
HEAD_V = 64
HEAD_VA = 80
MLA_Q_LORA = 384
MLA_KV_LORA = 256
MLA_NOPE = 64
MLA_ROPE = 32
DSA_HEAD_DIM = 64
DSA_ROT = 16
IDX_HEADS = 8
IDX_DIM = 64
DSA_TOPK_MAX = 256

N_GROUPS = 8
EXPERTS_PER_GROUP = 4
N_EXPERTS = N_GROUPS * EXPERTS_PER_GROUP
EXPERT_FF = 256

LANES = 128
INT_MIN = -(2 ** 31)
QK_LEAD = 1
MOE_TILE = 1024
MOE_SUB = 128
MOE_CHUNK = 512
VMEM_LIMIT = 52 * 1024 * 1024

_C_CQ, _C_CKV, _C_QB, _C_KB, _C_VB, _C_QI, _C_KI, _C_MISC, _C_END = (
    0, 384, 640, 1152, 1664, 2176, 2688, 2816, 2944)
_MISC_KR = 64
_MISC_W = 96


def _rms(x, g):
    return x * lax.rsqrt(jnp.mean(x * x, axis=-1, keepdims=True) + EPS) * g


def _tile_lanes(t, width):
    reps = width // LANES
    return t if reps == 1 else jnp.concatenate([t] * reps, axis=1)


def _rope(x, cos, sin_m, sin_p, half):
    w = x.shape[1]
    c, sm, sp = (_tile_lanes(t, w) for t in (cos, sin_m, sin_p))
    return x * c + pltpu.roll(x, w - half, 1) * sm + pltpu.roll(x, half, 1) * sp


def _aug_transpose(v):
    vt = v.T
    ones = jnp.ones((HEAD_VA - HEAD_V, vt.shape[1]), F32)
    parts = []
    for hh in range(N_HEADS):
        parts += [vt[hh * HEAD_V:(hh + 1) * HEAD_V, :], ones]
    return jnp.concatenate(parts, axis=0).astype(BF16)


def _inproj_kernel(x_ref, g_ref, w1_ref, wuq_ref, wuk_ref, wuv_ref, gq_ref, gkv_ref, tab_ref,
                   qa_ref, ka_ref, vaT_ref, qb_ref, kb_ref, vbT_ref, qi_ref, ki_ref, wT_ref,
                   *, scale_a):
    x = x_ref[0]
    h = _rms(x, g_ref[...]).astype(BF16)

    def proj(a, b):
        return jnp.dot(h, w1_ref[:, a:b], preferred_element_type=F32)

    cos_a, sin_am, sin_ap = tab_ref[0], tab_ref[1], tab_ref[2]
    cos_b, sin_bm, sin_bp = tab_ref[3], tab_ref[4], tab_ref[5]

    cq = _rms(proj(_C_CQ, _C_CKV), gq_ref[...]).astype(BF16)
    qa = jnp.dot(cq, wuq_ref[...], preferred_element_type=F32)
    qa = _rope(qa, cos_b, sin_bm, sin_bp, MLA_ROPE // 2) * scale_a
    ckv = _rms(proj(_C_CKV, _C_QB), gkv_ref[...]).astype(BF16)
    kn = jnp.dot(ckv, wuk_ref[...], preferred_element_type=F32)
    misc = _rope(proj(_C_MISC, _C_END), cos_b, sin_bm, sin_bp, MLA_ROPE // 2)
    lane = lax.broadcasted_iota(I32, misc.shape, 1)
    kpe = jnp.where((lane >= _MISC_KR) & (lane < _MISC_KR + MLA_ROPE), misc, 0.0)
    for hh in range(N_HEADS):
        sl = slice(hh * LANES, (hh + 1) * LANES)
        qa_ref[0, hh] = qa[:, sl].astype(BF16)
        ka_ref[0, hh] = (kn[:, sl] + kpe).astype(BF16)
    va = jnp.dot(ckv, wuv_ref[...], preferred_element_type=F32)
    vaT_ref[0] = _aug_transpose(va)

    qb = _rope(proj(_C_QB, _C_KB), cos_a, sin_am, sin_ap, DSA_ROT // 2) * (DSA_HEAD_DIM ** -0.5 * LOG2E)
    kb = _rope(proj(_C_KB, _C_VB), cos_a, sin_am, sin_ap, DSA_ROT // 2)
    qi = _rope(proj(_C_QI, _C_KI), cos_a, sin_am, sin_ap, DSA_ROT // 2) * (IDX_DIM ** -0.5)
    for j in range(N_HEADS // 2):
        sl = slice(j * LANES, (j + 1) * LANES)
        qb_ref[0, j] = qb[:, sl].astype(BF16)
        kb_ref[0, j] = kb[:, sl].astype(BF16)
        qi_ref[0, j] = qi[:, sl].astype(BF16)
    vbT_ref[0] = _aug_transpose(proj(_C_VB, _C_QI))
    ki2 = _rope(proj(_C_KI, _C_MISC), cos_a, sin_am, sin_ap, DSA_ROT // 2)
    ki_ref[0] = ki2.astype(BF16)
    wT_ref[0] = misc.T[_MISC_W:_MISC_W + IDX_HEADS, :] * (IDX_HEADS ** -0.5)


def _inproj_call(x, g_mix, w1, wuq, wuk, wuv, gq, gkv, tab, tm):
    b, s, d = x.shape
    nh, hp = N_HEADS, N_HEADS // 2
    full = lambda shape: pl.BlockSpec(shape, lambda bi, i: (0,) * len(shape))
    out_shape = (
        jax.ShapeDtypeStruct((b, nh, s, LANES), BF16),
        jax.ShapeDtypeStruct((b, nh, s, LANES), BF16),
        jax.ShapeDtypeStruct((b, nh * HEAD_VA, s), BF16),
        jax.ShapeDtypeStruct((b, hp, s, LANES), BF16),
        jax.ShapeDtypeStruct((b, hp, s, LANES), BF16),
        jax.ShapeDtypeStruct((b, nh * HEAD_VA, s), BF16),
        jax.ShapeDtypeStruct((b, hp, s, LANES), BF16),
        jax.ShapeDtypeStruct((b, s, LANES), BF16),
        jax.ShapeDtypeStruct((b, IDX_HEADS, s), F32),
    )
    head_spec = lambda n: pl.BlockSpec((1, n, tm, LANES), lambda bi, i: (bi, 0, i, 0))
    t_spec = pl.BlockSpec((1, nh * HEAD_VA, tm), lambda bi, i: (bi, 0, i))
    out_specs = (head_spec(nh), head_spec(nh), t_spec, head_spec(hp), head_spec(hp), t_spec,
                 head_spec(hp), pl.BlockSpec((1, tm, LANES), lambda bi, i: (bi, i, 0)),
                 pl.BlockSpec((1, IDX_HEADS, tm), lambda bi, i: (bi, 0, i)))
    in_specs = [
        pl.BlockSpec((1, tm, d), lambda bi, i: (bi, i, 0)),
        full(g_mix.shape), full(w1.shape), full(wuq.shape), full(wuk.shape), full(wuv.shape),
        full(gq.shape), full(gkv.shape),
        pl.BlockSpec((tab.shape[0], tm, LANES), lambda bi, i: (0, i, 0)),
    ]
    scale_a = (MLA_NOPE + MLA_ROPE) ** -0.5 * LOG2E
    return pl.pallas_call(
        functools.partial(_inproj_kernel, scale_a=scale_a),
        out_shape=out_shape, grid=(b, s // tm), in_specs=in_specs, out_specs=out_specs,
        compiler_params=pltpu.CompilerParams(
            dimension_semantics=("arbitrary", "arbitrary"), vmem_limit_bytes=VMEM_LIMIT),
        name="inproj",
    )(x, g_mix, w1, wuq, wuk, wuv, gq, gkv, tab)


def _head_scores(h, q_ref, k_ref, pairs):
    if pairs:
        q = q_ref[0, h // 2]
        lane = lax.broadcasted_iota(I32, q.shape, 1)
        q = jnp.where(lax.shift_right_logical(lane, 6) == (h % 2), q, jnp.zeros_like(q))
        k = k_ref[0, h // 2]
    else:
        q = q_ref[0, h]
        k = k_ref[0, h]
    return lax.dot_general(k, q, (((1,), (1,)), ((), ())), preferred_element_type=F32)


def _flash_heads(q_ref, k_ref, vT_ref, m_scr, acc_scr, pairs, mask):
    lead = QK_LEAD
    pending = [_head_scores(h, q_ref, k_ref, pairs) for h in range(lead)]
    for h in range(N_HEADS):
        if h + lead < N_HEADS:
            pending.append(_head_scores(h + lead, q_ref, k_ref, pairs))
        _flash_update(h, pending.pop(0), vT_ref, m_scr, acc_scr, mask)


def _flash_update(h, s, vT_ref, m_scr, acc_scr, mask):
    if mask is not None:
        s = jnp.where(mask, s, NEG)
    row = slice(h, h + 1)
    m_prev = m_scr[row, :]
    m_new = jnp.maximum(m_prev, jnp.max(s, axis=0, keepdims=True))
    p = jnp.exp2(s - m_new).astype(BF16)
    alpha = jnp.exp2(m_prev - m_new)
    m_scr[row, :] = m_new
    rows = slice(h * HEAD_VA, (h + 1) * HEAD_VA)
    pv = jnp.dot(vT_ref[0, rows, :], p, preferred_element_type=F32)
    acc_scr[rows, :] = alpha * acc_scr[rows, :] + pv


def _flash_init(m_scr, acc_scr):
    m_scr[...] = jnp.full(m_scr.shape, NEG, F32)
    acc_scr[...] = jnp.zeros(acc_scr.shape, F32)


def _flash_finalize(o_ref, acc_scr):
    parts = []
    for hh in range(N_HEADS):
        r0 = hh * HEAD_VA
        parts.append(acc_scr[r0:r0 + HEAD_V, :] * (1.0 / acc_scr[r0 + HEAD_V:r0 + HEAD_V + 1, :]))
    o_ref[0] = jnp.concatenate(parts, axis=0).T.astype(BF16)


def _causal_mask(qi, ki, tq, tk):
    s_idx = ki * tk + lax.broadcasted_iota(I32, (tk, tq), 0)
    t_idx = qi * tq + lax.broadcasted_iota(I32, (tk, tq), 1)
    return lax.shift_right_logical(s_idx, CHUNK_SHIFT) <= lax.shift_right_logical(t_idx, CHUNK_SHIFT)


def _kmax(qi, tq, tk):
    return ((qi + 1) * tq - 1) // tk


def _mla_kernel(qi_tab, ki_tab, q_ref, k_ref, vT_ref, o_ref, m_scr, acc_scr, *, tq, tk):
    p = pl.program_id(1)
    qi = qi_tab[p]
    ki = ki_tab[p]

    @pl.when(ki == 0)
    def _():
        _flash_init(m_scr, acc_scr)

    diag = (ki + 1) * tk > qi * tq

    def run(masked):
        mask = _causal_mask(qi, ki, tq, tk) if masked else None

        _flash_heads(q_ref, k_ref, vT_ref, m_scr, acc_scr, False, mask)

    pl.when(diag)(lambda: run(True))
    pl.when(jnp.logical_not(diag))(lambda: run(False))

    @pl.when(ki == _kmax(qi, tq, tk))
    def _():
        _flash_finalize(o_ref, acc_scr)


def _mla_call(qa, ka, vaT, tq, tk):
    b, nh, s, _ = qa.shape
    nq = s // tq
    qi_l, ki_l = [], []
    for qi in range(nq):
        for ki in range(((qi + 1) * tq - 1) // tk + 1):
            qi_l.append(qi)
            ki_l.append(ki)
    qi_tab = jnp.asarray(np.array(qi_l, np.int32))
    ki_tab = jnp.asarray(np.array(ki_l, np.int32))
    grid_spec = pltpu.PrefetchScalarGridSpec(
        num_scalar_prefetch=2, grid=(b, len(qi_l)),
        in_specs=[
            pl.BlockSpec((1, nh, tq, LANES), lambda bi, p, qt, kt: (bi, 0, qt[p], 0)),
            pl.BlockSpec((1, nh, tk, LANES), lambda bi, p, qt, kt: (bi, 0, kt[p], 0)),
            pl.BlockSpec((1, nh * HEAD_VA, tk), lambda bi, p, qt, kt: (bi, 0, kt[p])),
        ],
        out_specs=pl.BlockSpec((1, tq, nh * HEAD_V), lambda bi, p, qt, kt: (bi, qt[p], 0)),
        scratch_shapes=[pltpu.VMEM((nh, tq), F32), pltpu.VMEM((nh * HEAD_VA, tq), F32)],
    )
    return pl.pallas_call(
        functools.partial(_mla_kernel, tq=tq, tk=tk),
        out_shape=jax.ShapeDtypeStruct((b, s, nh * HEAD_V), BF16),
        grid_spec=grid_spec,
        compiler_params=pltpu.CompilerParams(
            dimension_semantics=("arbitrary", "arbitrary"), vmem_limit_bytes=VMEM_LIMIT),
        name="mla_attn",
    )(qi_tab, ki_tab, qa, ka, vaT)


_F_LAST_A, _F_PHASE_C, _F_FIRST_C, _F_LAST_C = 1, 2, 4, 8


KEY_LOWEST_FINITE = -2139095040


def _key_to_f32(k):
    return lax.bitcast_convert_type(k ^ (lax.shift_right_arithmetic(k, 31) & 0x7FFFFFFF), F32)


def _trunc_bf16(x):
    hi = lax.bitcast_convert_type(x, I32) & jnp.int32(-65536)
    return lax.bitcast_convert_type(hi, F32).astype(BF16)


def _count(sc_scr, nkb, trial, tq, tk, strict=False):
    def body(j, acc):
        sc = sc_scr[j]
        hit = ((sc > trial) if strict else (sc >= trial)).astype(I32)
        return acc + jnp.sum(hit.reshape(tk // 8, 8, tq), axis=0)
    acc = lax.fori_loop(0, nkb, body, jnp.zeros((8, tq), I32))
    return jnp.sum(acc, axis=0, keepdims=True)


def _min_above(sc_scr, nkb, bound, tq, tk, inclusive):
    def body(j, acc):
        sc = sc_scr[j]
        val = jnp.where((sc >= bound) if inclusive else (sc > bound), sc, jnp.inf)
        return jnp.minimum(acc, jnp.min(val.reshape(tk // 8, 8, tq), axis=0))
    acc = lax.fori_loop(0, nkb, body, jnp.full((8, tq), jnp.inf, F32))
    return jnp.min(acc, axis=0, keepdims=True)


def _count16(t16_scr, nkb, trial16, tq, tk):
    one = jnp.ones((), BF16)
    zero = jnp.zeros((), BF16)

    def body(j, acc):
        hit = jnp.where(t16_scr[j] >= trial16, one, zero)
        part = hit[0:16, :]
        for r in range(1, tk // 16):
            part = part + hit[r * 16:(r + 1) * 16, :]
        return acc + part.astype(F32)
    acc = lax.fori_loop(0, nkb, body, jnp.zeros((16, tq), F32))
    return jnp.sum(acc, axis=0, keepdims=True).astype(I32)


def _count_tied_before(sc_scr, nkb, thr, bound, tq, tk):
    def body(j, acc):
        idx = j * tk + lax.broadcasted_iota(I32, (tk, tq), 0)
        hit = jnp.where(sc_scr[j] == thr, (idx < bound).astype(I32), 0)
        return acc + jnp.sum(hit.reshape(tk // 8, 8, tq), axis=0)
    acc = lax.fori_loop(0, nkb, body, jnp.zeros((8, tq), I32))
    return jnp.sum(acc, axis=0, keepdims=True)


def _select_threshold(sc_scr, t16_scr, nkb, cand_scr, cnt_scr, thr_scr, top_k, s_len, tq, tk):
    cand_scr[...] = jnp.full((1, tq), INT_MIN, I32)
    cnt_scr[...] = jnp.full((1, tq), top_k + 1, I32)

    def make_step(counter):
        def step(c):
            bit, _ = c
            cand = cand_scr[...]
            trial = cand + lax.shift_left(jnp.int32(1), bit)
            cnt = counter(trial)
            ok = cnt >= top_k
            cand_scr[...] = jnp.where(ok, trial, cand)
            new_cnt = jnp.where(ok, cnt, cnt_scr[...])
            cnt_scr[...] = new_cnt
            pending = jnp.max(jnp.where(new_cnt == top_k, 0.0, 1.0))
            return bit - 1, (pending == 0.0).astype(I32)
        return step

    lax.while_loop(lambda c: c[0] >= 16,
                   make_step(lambda k: _count16(t16_scr, nkb, _trunc_bf16(_key_to_f32(k)), tq, tk)),
                   (jnp.int32(31), jnp.int32(0)))
    c16 = cand_scr[...]
    kp = jnp.where(c16 >= 0, c16, c16 | 0xFFFF)
    cand_scr[...] = kp - 0x8002
    cnt_scr[...] = jnp.full((1, tq), top_k + 1, I32)
    lax.while_loop(lambda c: jnp.logical_and(c[0] >= 0, c[1] == 0),
                   make_step(lambda k: _count(sc_scr, nkb, _key_to_f32(k), tq, tk)),
                   (jnp.int32(16), jnp.int32(0)))

    cand = cand_scr[...]
    thr_grid = _key_to_f32(jnp.maximum(cand, KEY_LOWEST_FINITE))
    thr_scr[...] = thr_grid
    inexact = jnp.logical_and(cnt_scr[...] != top_k, cand >= KEY_LOWEST_FINITE)

    @pl.when(jnp.max(inexact.astype(I32)) > 0)
    def _():
        def still_above(kth, active):
            more = jnp.logical_and(active > 0.0, _count(sc_scr, nkb, kth, tq, tk, strict=True) >= top_k)
            return more.astype(F32)

        def walk(c):
            kth, active, _ = c
            kth = jnp.where(active > 0.0, _min_above(sc_scr, nkb, kth, tq, tk, False), kth)
            active = still_above(kth, active)
            return kth, active, (jnp.max(active) > 0.0).astype(I32)
        kth0 = _min_above(sc_scr, nkb, thr_grid, tq, tk, True)
        act0 = still_above(kth0, inexact.astype(F32))
        kth, _, _ = lax.while_loop(lambda c: c[2] > 0, walk,
                                   (kth0, act0, (jnp.max(act0) > 0.0).astype(I32)))
        thr = jnp.where(inexact, kth, thr_grid)
        thr_scr[...] = thr
        need = top_k - _count(sc_scr, nkb, thr, tq, tk, strict=True)
        cnt_scr[...] = jnp.zeros((1, tq), I32)
        nbits = max(1, int(math.ceil(math.log2(s_len))))

        def jstep(i, c):
            bit = nbits - 1 - i
            j0 = cnt_scr[...]
            trial = j0 + lax.shift_left(jnp.int32(1), bit)
            below = _count_tied_before(sc_scr, nkb, thr, trial, tq, tk)
            cnt_scr[...] = jnp.where(below < need, trial, j0)
            return c
        lax.fori_loop(0, nbits, jstep, 0)
        j0 = jnp.where(inexact, cnt_scr[...], s_len)

        def demote(j, c):
            idx = j * tk + lax.broadcasted_iota(I32, (tk, tq), 0)
            sj = sc_scr[j]
            sc_scr[j] = jnp.where(jnp.logical_and(sj == thr, idx > j0), -jnp.inf, sj)
            return c
        lax.fori_loop(0, nkb, demote, 0)


def _dsa_kernel(qi_tab, ki_tab, kia_tab, kic_tab, fl_tab,
                qi_ref, ki_ref, wT_ref, qb_ref, kb_ref, vbT_ref, o_ref,
                sc_scr, t16_scr, cand_scr, cnt_scr, thr_scr, m_scr, acc_scr,
                *, tq, tk, top_k, s_len):
    p = pl.program_id(1)
    qi = qi_tab[p]
    ki = ki_tab[p]
    fl = fl_tab[p]

    @pl.when((fl & _F_PHASE_C) == 0)
    def _():
        kidx = ki_ref[0]

        score = None
        for h in range(IDX_HEADS):
            q = qi_ref[0, h // 2]
            lane = lax.broadcasted_iota(I32, q.shape, 1)
            q = jnp.where(lax.shift_right_logical(lane, 6) == (h % 2), q, jnp.zeros_like(q))
            lg = lax.dot_general(kidx, q, (((1,), (1,)), ((), ())), preferred_element_type=F32)
            t = wT_ref[0, h:h + 1, :] * jnp.maximum(lg, 0.0)
            score = t if score is None else score + t
        score = jnp.where(_causal_mask(qi, ki, tq, tk), score, -jnp.inf)
        sc_scr[ki] = score
        t16_scr[ki] = score.astype(BF16)

        @pl.when((fl & _F_LAST_A) != 0)
        def _():
            _select_threshold(sc_scr, t16_scr, ki + 1, cand_scr, cnt_scr, thr_scr, top_k, s_len, tq, tk)

    @pl.when((fl & _F_PHASE_C) != 0)
    def _():
        @pl.when((fl & _F_FIRST_C) != 0)
        def _():
            _flash_init(m_scr, acc_scr)

        mask = sc_scr[ki] >= thr_scr[...]

        _flash_heads(qb_ref, kb_ref, vbT_ref, m_scr, acc_scr, True, mask)

        @pl.when((fl & _F_LAST_C) != 0)
        def _():
            _flash_finalize(o_ref, acc_scr)


def _dsa_call(qi_arr, ki_arr, wT, qb, kb, vbT, tq, tk, top_k):
    b, hp, s, _ = qb.shape
    nq, nkb = s // tq, s // tk
    qi_l, ki_l, kia_l, kic_l, fl_l = [], [], [], [], []
    for qi in range(nq):
        kmax = ((qi + 1) * tq - 1) // tk
        for ki in range(kmax + 1):
            qi_l.append(qi); ki_l.append(ki); kia_l.append(ki); kic_l.append(0)
            fl_l.append(_F_LAST_A if ki == kmax else 0)
        for ki in range(kmax + 1):
            qi_l.append(qi); ki_l.append(ki); kia_l.append(kmax); kic_l.append(ki)
            fl_l.append(_F_PHASE_C | (_F_FIRST_C if ki == 0 else 0) | (_F_LAST_C if ki == kmax else 0))
    tabs = [jnp.asarray(np.array(t, np.int32)) for t in (qi_l, ki_l, kia_l, kic_l, fl_l)]
    nh = N_HEADS
    grid_spec = pltpu.PrefetchScalarGridSpec(
        num_scalar_prefetch=5, grid=(b, len(qi_l)),
        in_specs=[
            pl.BlockSpec((1, hp, tq, LANES), lambda bi, p, qt, kt, ka, kc, f: (bi, 0, qt[p], 0)),
            pl.BlockSpec((1, tk, LANES), lambda bi, p, qt, kt, ka, kc, f: (bi, ka[p], 0)),
            pl.BlockSpec((1, IDX_HEADS, tq), lambda bi, p, qt, kt, ka, kc, f: (bi, 0, qt[p])),
            pl.BlockSpec((1, hp, tq, LANES), lambda bi, p, qt, kt, ka, kc, f: (bi, 0, qt[p], 0)),
            pl.BlockSpec((1, hp, tk, LANES), lambda bi, p, qt, kt, ka, kc, f: (bi, 0, kc[p], 0)),
            pl.BlockSpec((1, nh * HEAD_VA, tk), lambda bi, p, qt, kt, ka, kc, f: (bi, 0, kc[p])),
        ],
        out_specs=pl.BlockSpec((1, tq, nh * HEAD_V), lambda bi, p, qt, kt, ka, kc, f: (bi, qt[p], 0)),
        scratch_shapes=[
            pltpu.VMEM((nkb, tk, tq), F32),
            pltpu.VMEM((nkb, tk, tq), BF16),
            pltpu.VMEM((1, tq), I32), pltpu.VMEM((1, tq), I32), pltpu.VMEM((1, tq), F32),
            pltpu.VMEM((nh, tq), F32), pltpu.VMEM((nh * HEAD_VA, tq), F32),
        ],
    )
    return pl.pallas_call(
        functools.partial(_dsa_kernel, tq=tq, tk=tk, top_k=top_k, s_len=s),
        out_shape=jax.ShapeDtypeStruct((b, s, nh * HEAD_V), BF16),
        grid_spec=grid_spec,
        compiler_params=pltpu.CompilerParams(
            dimension_semantics=("arbitrary", "arbitrary"), vmem_limit_bytes=VMEM_LIMIT),
        name="dsa_attn",
    )(*tabs, qi_arr, ki_arr, wT, qb, kb, vbT)


_R_GROUP0, _R_EXP0 = 0, N_GROUPS


def _merge_kernel(x_ref, g_ref, wg_ref, oa_ref, ob_ref, woa_ref, wob_ref, wout_ref, gf_ref,
                  wr_ref, br_ref, x1_ref, h2_ref, comb_ref, gT_ref):
    x = x_ref[0]
    h = _rms(x, g_ref[...]).astype(BF16)
    gates = jnp.dot(h, wg_ref[...], preferred_element_type=F32)
    ya = jnp.dot(oa_ref[0], woa_ref[...], preferred_element_type=F32)
    yb = jnp.dot(ob_ref[0], wob_ref[...], preferred_element_type=F32)
    y = jax.nn.sigmoid(gates[:, :D_MODEL]) * ya + jax.nn.sigmoid(gates[:, D_MODEL:]) * yb
    x1 = x + jnp.dot(y.astype(BF16), wout_ref[...], preferred_element_type=F32)
    x1_ref[0] = x1
    h2 = _rms(x1, gf_ref[...])
    h2_ref[0] = h2.astype(BF16)

    h2_hi = h2.astype(BF16)
    h2_lo = (h2 - h2_hi.astype(F32)).astype(BF16)
    part = jnp.dot(h2_hi, wr_ref[...], preferred_element_type=F32)
    logits = (part[:, :LANES] + part[:, LANES:]
              + jnp.dot(h2_lo, wr_ref[:, :LANES], preferred_element_type=F32)) + br_ref[...]
    lane = lax.broadcasted_iota(I32, logits.shape, 1)
    ninf = -jnp.inf
    is_g = lane < N_GROUPS
    gl = jnp.where(is_g, logits, ninf)
    gmax = jnp.max(gl, axis=-1, keepdims=True)
    g_sel = jnp.min(jnp.where(gl == gmax, lane, LANES), axis=-1, keepdims=True)
    w_grp = 1.0 / jnp.sum(jnp.where(is_g, jnp.exp(gl - gmax), 0.0), axis=-1, keepdims=True)
    e_lane = lane - _R_EXP0
    in_grp = (e_lane >= 0) & (e_lane < N_EXPERTS) & (lax.shift_right_arithmetic(e_lane, 2) == g_sel)
    el = jnp.where(in_grp, logits, ninf)
    v1 = jnp.max(el, axis=-1, keepdims=True)
    i1 = jnp.min(jnp.where(el == v1, lane, LANES), axis=-1, keepdims=True)
    el2 = jnp.where(lane == i1, ninf, el)
    v2 = jnp.max(el2, axis=-1, keepdims=True)
    i2 = jnp.min(jnp.where(el2 == v2, lane, LANES), axis=-1, keepdims=True)
    e2 = jnp.exp(v2 - v1)
    w1 = 1.0 / (1.0 + e2)
    w2 = e2 * w1
    comb = jnp.where(lane == i1, w1 * w_grp, 0.0) + jnp.where(lane == i2, w2 * w_grp, 0.0)
    comb = comb + jnp.where(lane == 0, g_sel.astype(F32), 0.0)
    comb_ref[0] = comb
    gT_ref[0] = comb.T[0:8, :]


def _merge_call(x, g_mix, wg, oa, ob, woa, wob, wout, g_ffn, wr, br, tm):
    b, s, d = x.shape
    full = lambda a: pl.BlockSpec(a.shape, lambda bi, i: (0,) * a.ndim)
    tok = lambda w: pl.BlockSpec((1, tm, w), lambda bi, i: (bi, i, 0))
    return pl.pallas_call(
        _merge_kernel,
        out_shape=(jax.ShapeDtypeStruct((b, s, d), F32), jax.ShapeDtypeStruct((b, s, d), BF16),
                   jax.ShapeDtypeStruct((b, s, LANES), F32), jax.ShapeDtypeStruct((b, 8, s), F32)),
        grid=(b, s // tm),
        in_specs=[tok(d), full(g_mix), full(wg), tok(N_HEADS * HEAD_V), tok(N_HEADS * HEAD_V),
                  full(woa), full(wob), full(wout), full(g_ffn), full(wr), full(br)],
        out_specs=(tok(d), tok(d), tok(LANES), pl.BlockSpec((1, 8, tm), lambda bi, i: (bi, 0, i))),
        compiler_params=pltpu.CompilerParams(
            dimension_semantics=("arbitrary", "arbitrary"), vmem_limit_bytes=VMEM_LIMIT),
        name="merge_router",
    )(x, g_mix, wg, oa, ob, woa, wob, wout, g_ffn, wr, br)


def _moe_kernel(h2_ref, comb_ref, gT_ref, utri_ref, wgate_ref, wup_ref, wdown_ref, o_ref,
                xs_scr, cs_scr, ys_scr, pt_scr, meta_ref):
    i = pl.program_id(0)
    g = pl.program_id(1)
    tmo = h2_ref.shape[0]
    ns = xs_scr.shape[0]

    @pl.when(jnp.logical_and(i == 0, g == 0))
    def _():
        ys_scr[...] = jnp.zeros(ys_scr.shape, BF16)

    @pl.when(g == 0)
    def _():
        grow = gT_ref[0, 0:1, :]
        sub = lax.broadcasted_iota(I32, (16, tmo), 0).astype(F32)
        onehot = sub == grow
        ct = jnp.dot(jnp.where(onehot, 1.0, 0.0).astype(BF16), utri_ref[...],
                     preferred_element_type=F32)
        n_col = ct[:, tmo - 1:tmo]
        npad_col = jnp.floor((n_col + (MOE_SUB - 1)) * (1.0 / MOE_SUB)) * MOE_SUB
        bases, run = [], jnp.zeros((1, 1), F32)
        for r in range(N_GROUPS):
            bases.append(run)
            meta_ref[r] = run[0, 0].astype(I32)
            meta_ref[N_GROUPS + r] = (npad_col[r:r + 1, :] * (1.0 / MOE_SUB))[0, 0].astype(I32)
            run = run + npad_col[r:r + 1, :]
        meta_ref[2 * N_GROUPS] = run[0, 0].astype(I32)
        base_col = jnp.concatenate(bases + [jnp.zeros((16 - N_GROUPS, 1), F32)], axis=0)
        dest_row = jnp.sum(jnp.where(onehot, ct - 1.0 + base_col, 0.0), axis=0, keepdims=True)

        comb = comb_ref[...]
        c1 = comb.astype(BF16)
        c2 = (comb - c1.astype(F32)).astype(BF16)
        c3 = (comb - c1.astype(F32) - c2.astype(F32)).astype(BF16)
        rhs = jnp.concatenate([h2_ref[...], c1, c2, c3], axis=1)
        for cb in range(ns // MOE_CHUNK):
            @pl.when(cb * MOE_CHUNK < meta_ref[2 * N_GROUPS])
            def _():
                slot = (cb * MOE_CHUNK + lax.broadcasted_iota(I32, (MOE_CHUNK, tmo), 0)).astype(F32)
                p = jnp.where(slot == dest_row, 1.0, 0.0).astype(BF16)
                got = jnp.dot(p, rhs, preferred_element_type=F32)
                rows = slice(cb * MOE_CHUNK, (cb + 1) * MOE_CHUNK)
                xs_scr[rows, :] = got[:, :D_MODEL].astype(BF16)
                cs_scr[rows, :] = (got[:, D_MODEL:D_MODEL + LANES] + got[:, D_MODEL + LANES:D_MODEL + 2 * LANES]
                                   + got[:, D_MODEL + 2 * LANES:])
        dest_col = jnp.broadcast_to(dest_row, (LANES, tmo)).T
        for cb in range(ns // LANES):
            slot = (cb * LANES + lax.broadcasted_iota(I32, (tmo, LANES), 1)).astype(F32)
            pt_scr[:, cb * LANES:(cb + 1) * LANES] = jnp.where(slot == dest_col, 1.0, 0.0).astype(BF16)

    base = meta_ref[g]
    lane = lax.broadcasted_iota(I32, (MOE_SUB, LANES), 1)

    def block(j, c):
        rows = pl.ds(pl.multiple_of(base + j * MOE_SUB, MOE_SUB), MOE_SUB)
        x = xs_scr[rows, :]
        cw = cs_scr[rows, :]
        hids = []
        for e in range(EXPERTS_PER_GROUP):
            a = jnp.dot(x, wgate_ref[e], preferred_element_type=F32)
            u = jnp.dot(x, wup_ref[e], preferred_element_type=F32)
            ce = jnp.sum(jnp.where(lane == _R_EXP0 + g * EXPERTS_PER_GROUP + e, cw, 0.0),
                         axis=-1, keepdims=True)
            hids.append((a * jax.nn.sigmoid(a) * u * ce).astype(BF16))
        wd = wdown_ref[...].reshape(EXPERTS_PER_GROUP * EXPERT_FF, D_MODEL)
        y = jnp.dot(jnp.concatenate(hids, axis=1), wd, preferred_element_type=F32)
        ys_scr[rows, :] = y.astype(BF16)
        return c
    lax.fori_loop(0, meta_ref[N_GROUPS + g], block, 0)

    @pl.when(g == N_GROUPS - 1)
    def _():
        o_ref[...] = jnp.dot(pt_scr[...], ys_scr[...], preferred_element_type=F32).astype(BF16)


def _final_kernel(x1_ref, m_ref, g_ref, o_ref):
    o_ref[...] = _rms(x1_ref[...] + m_ref[...].astype(F32), g_ref[...])


def _moe_call(h2, comb, gT, wgate, wup, wdown, tmo):
    t, d = h2.shape
    epg = EXPERTS_PER_GROUP
    ns = tmo + N_GROUPS * MOE_SUB
    ns = -(-ns // MOE_CHUNK) * MOE_CHUNK
    utri = (np.arange(tmo)[:, None] <= np.arange(tmo)[None, :]).astype(np.float32)
    return pl.pallas_call(
        _moe_kernel,
        out_shape=jax.ShapeDtypeStruct((t, d), BF16),
        grid=(t // tmo, N_GROUPS),
        in_specs=[
            pl.BlockSpec((tmo, d), lambda i, g: (i, 0)),
            pl.BlockSpec((tmo, LANES), lambda i, g: (i, 0)),
            pl.BlockSpec((1, 8, tmo), lambda i, g: (0, 0, i)),
            pl.BlockSpec((tmo, tmo), lambda i, g: (0, 0)),
            pl.BlockSpec((epg, d, EXPERT_FF), lambda i, g: (g, 0, 0)),
            pl.BlockSpec((epg, d, EXPERT_FF), lambda i, g: (g, 0, 0)),
            pl.BlockSpec((epg, EXPERT_FF, d), lambda i, g: (g, 0, 0)),
        ],
        out_specs=pl.BlockSpec((tmo, d), lambda i, g: (i, 0)),
        scratch_shapes=[pltpu.VMEM((ns, d), BF16), pltpu.VMEM((ns, LANES), F32), pltpu.VMEM((ns, d), BF16),
                        pltpu.VMEM((tmo, ns), BF16), pltpu.SMEM((2 * N_GROUPS + 1,), I32)],
        compiler_params=pltpu.CompilerParams(
            dimension_semantics=("arbitrary", "arbitrary"), vmem_limit_bytes=VMEM_LIMIT),
        name="moe_sorted",
    )(h2, comb, gT.reshape(1, 8, t), jnp.asarray(utri, BF16), wgate, wup, wdown)


def _final_norm_tile(t):
    return min(512, t)


def _final_call(x1, m, g_fin, tm):
    t, d = x1.shape
    return pl.pallas_call(
        _final_kernel,
        out_shape=jax.ShapeDtypeStruct((t, d), F32),
        grid=(t // tm,),
        in_specs=[pl.BlockSpec((tm, d), lambda i: (i, 0)), pl.BlockSpec((tm, d), lambda i: (i, 0)),
                  pl.BlockSpec((1, d), lambda i: (0, 0))],
        out_specs=pl.BlockSpec((tm, d), lambda i: (i, 0)),
        compiler_params=pltpu.CompilerParams(dimension_semantics=("arbitrary",)),
        name="final_norm",
    )(x1, m, g_fin)


def _rope_tables(s):
    pos = jnp.arange(s, dtype=F32)[:, None]
    lane = np.arange(LANES)

    def build(rot, lane_in_rot, first, second):
        half = rot // 2
        inv_freq = ROPE_THETA ** (-jnp.arange(half, dtype=F32) * 2.0 / rot)
        ang = pos * inv_freq[None, :]
        cos_h, sin_h = jnp.cos(ang), jnp.sin(ang)
        idx = np.where(first | second, lane_in_rot % half, 0)
        cos = jnp.where(first | second, cos_h[:, idx], 1.0)
        sin_m = jnp.where(first, -sin_h[:, idx], 0.0)
        sin_p = jnp.where(second, sin_h[:, idx], 0.0)
        return [cos, sin_m, sin_p]

    l64 = lane % 64
    a = build(DSA_ROT, l64, l64 < DSA_ROT // 2, (l64 >= DSA_ROT // 2) & (l64 < DSA_ROT))
    lb = lane - _MISC_KR
    bt = build(MLA_ROPE, np.maximum(lb, 0), (lb >= 0) & (lb < MLA_ROPE // 2),
               (lb >= MLA_ROPE // 2) & (lb < MLA_ROPE))
    return jnp.stack(a + bt).astype(F32)


def _pad_heads(w, n_heads, width):
    k = w.shape[0]
    w = w.reshape(k, n_heads, width)
    return jnp.pad(w, ((0, 0), (0, 0), (0, LANES - width))).reshape(k, n_heads * LANES)


def _tile_sizes(s):
    tm = min(256, s)
    tq = min(512, s)
    tk = min(512, s)
    return tm, tq, tk


def kernel(x, norm_mix_g, w_in, mla_q_norm_g, w_uq, mla_kv_norm_g, w_uk, w_uv, w_o_a, w_o_b, w_out,
           norm_ffn_g, w_router_group, b_router_group, w_router_expert, b_router_expert,
           w_gate, w_up, w_down, final_norm_g):
    b, s, d = x.shape
    assert d == D_MODEL and s % CHUNK == 0
    tm, tq, tk = _tile_sizes(s)
    assert s % tm == 0 and s % tq == 0 and s % tk == 0 and tq % CHUNK == 0 and tk >= DSA_TOPK_MAX
    top_k = min(DSA_TOPK_MAX, s // 4)
    tab = _rope_tables(s)
    depth = w_in.shape[0]
    assert depth == 1, "the final norm is fused into the MoE kernel of the single layer"
    row = lambda v: v.reshape(1, -1).astype(F32)

    for l in range(depth):
        sizes = (MLA_Q_LORA, MLA_KV_LORA, MLA_ROPE, 512, 512, 512, 512, IDX_DIM, IDX_HEADS, D_MODEL, D_MODEL)
        offs = np.concatenate([[0], np.cumsum(sizes)])
        seg = [w_in[l][:, offs[i]:offs[i + 1]] for i in range(len(sizes))]
        (w_cq, w_ckv, w_kr, w_qb, w_kb, w_vb, w_qi, w_ki, w_wi, w_ga, w_gb) = seg
        misc = jnp.concatenate([jnp.zeros((d, _MISC_KR), F32), w_kr, w_wi,
                                jnp.zeros((d, LANES - _MISC_W - IDX_HEADS), F32)], axis=1)
        w1 = jnp.concatenate([w_cq, w_ckv, w_qb, w_kb, w_vb, w_qi, w_ki, w_ki, misc], axis=1).astype(BF16)
        wg = jnp.concatenate([w_ga, w_gb], axis=1).astype(BF16)
        wuq = _pad_heads(w_uq[l], N_HEADS, MLA_NOPE + MLA_ROPE).astype(BF16)
        wuk = _pad_heads(w_uk[l], N_HEADS, MLA_NOPE).astype(BF16)
        wuv = w_uv[l].astype(BF16)

        qa, ka, vaT, qb, kb, vbT, qi, ki, wT = _inproj_call(
            x, row(norm_mix_g[l]), w1, wuq, wuk, wuv, row(mla_q_norm_g[l]), row(mla_kv_norm_g[l]), tab, tm)
        o_a = _mla_call(qa, ka, vaT, tq, tk)
        o_b = _dsa_call(qi, ki, wT, qb, kb, vbT, tq, tk, top_k)

        wr = jnp.concatenate([w_router_group[l], w_router_expert[l],
                              jnp.zeros((d, LANES - N_GROUPS - N_EXPERTS), F32)], axis=1)
        wr_hi = wr.astype(BF16)
        wr = jnp.concatenate([wr_hi, (wr - wr_hi.astype(F32)).astype(BF16)], axis=1)
        br = jnp.concatenate([b_router_group[l], b_router_expert[l],
                              jnp.zeros((LANES - N_GROUPS - N_EXPERTS,), F32)]).reshape(1, LANES)
        x1, h2, comb, gT = _merge_call(x, row(norm_mix_g[l]), wg, o_a, o_b, w_o_a[l].astype(BF16),
                                       w_o_b[l].astype(BF16), w_out[l].astype(BF16), row(norm_ffn_g[l]), wr, br, tm)
        t = b * s
        tmo = min(MOE_TILE, t)
        assert t % tmo == 0
        moe = _moe_call(h2.reshape(t, d), comb.reshape(t, LANES), gT.transpose(1, 0, 2).reshape(8, t),
                        w_gate[l].astype(BF16), w_up[l].astype(BF16), w_down[l].astype(BF16), tmo)
        y = _final_call(x1.reshape(t, d), moe, row(final_norm_g), _final_norm_tile(t))
        x = y.reshape(b, s, d)
    return x
```

```python
import functools
import math

import numpy as np
import jax
import jax.numpy as jnp
from jax import lax
from jax.experimental import pallas as pl
from jax.experimental.pallas import tpu as pltpu

F32 = jnp.float32
BF16 = jnp.bfloat16
I32 = jnp.int32

D_MODEL = 1024
CHUNK = 64
CHUNK_SHIFT = 6
ROPE_THETA = 500000.0
EPS = 1e-6
NEG = -1e30
LOG2E = 1.4426950408889634

N_HEADS = 8
HEAD_V = 64
HEAD_VA = 80
MLA_Q_LORA = 384
MLA_KV_LORA = 256
MLA_NOPE = 64
MLA_ROPE = 32
DSA_HEAD_DIM = 64
DSA_ROT = 16
IDX_HEADS = 8
IDX_DIM = 64
DSA_TOPK_MAX = 256

N_GROUPS = 8
EXPERTS_PER_GROUP = 4
N_EXPERTS = N_GROUPS * EXPERTS_PER_GROUP
EXPERT_FF = 256

LANES = 128
INT_MIN = -(2 ** 31)
Q_SPLIT = 2
MLA_QK_LEAD = 4
DSA_QK_LEAD = 4
MOE_TILE = 1024
MOE_SUB = 128
MOE_CHUNK = 512
VMEM_LIMIT = 52 * 1024 * 1024

_C_CQ, _C_CKV, _C_QB, _C_KB, _C_VB, _C_QI, _C_KI, _C_MISC, _C_END = (
    0, 384, 640, 1152, 1664, 2176, 2688, 2816, 2944)
_MISC_KR = 64
_MISC_W = 96


def _rms(x, g):
    return x * lax.rsqrt(jnp.mean(x * x, axis=-1, keepdims=True) + EPS) * g


def _tile_lanes(t, width):
    reps = width // LANES
    return t if reps == 1 else jnp.concatenate([t] * reps, axis=1)


def _rope(x, cos, sin_m, sin_p, half):
    w = x.shape[1]
    c, sm, sp = (_tile_lanes(t, w) for t in (cos, sin_m, sin_p))
    return x * c + pltpu.roll(x, w - half, 1) * sm + pltpu.roll(x, half, 1) * sp


def _aug_transpose(v):
    vt = v.T
    ones = jnp.ones((HEAD_VA - HEAD_V, vt.shape[1]), F32)
    parts = []
    for hh in range(N_HEADS):
        parts += [vt[hh * HEAD_V:(hh + 1) * HEAD_V, :], ones]
    return jnp.concatenate(parts, axis=0).astype(BF16)


def _inproj_kernel(x_ref, g_ref, w1_ref, wuq_ref, wuk_ref, wuv_ref, gq_ref, gkv_ref, tab_ref,
                   qa_ref, ka_ref, vaT_ref, qb_ref, kb_ref, vbT_ref, qi_ref, ki_ref, wT_ref,
                   *, scale_a):
    x = x_ref[0]
    h = _rms(x, g_ref[...]).astype(BF16)

    def proj(a, b):
        return jnp.dot(h, w1_ref[:, a:b], preferred_element_type=F32)

    cos_a, sin_am, sin_ap = tab_ref[0], tab_ref[1], tab_ref[2]
    cos_b, sin_bm, sin_bp = tab_ref[3], tab_ref[4], tab_ref[5]

    cq = _rms(proj(_C_CQ, _C_CKV), gq_ref[...]).astype(BF16)
    qa = jnp.dot(cq, wuq_ref[...], preferred_element_type=F32)
    qa = _rope(qa, cos_b, sin_bm, sin_bp, MLA_ROPE // 2) * scale_a
    ckv = _rms(proj(_C_CKV, _C_QB), gkv_ref[...]).astype(BF16)
    kn = jnp.dot(ckv, wuk_ref[...], preferred_element_type=F32)
    misc = _rope(proj(_C_MISC, _C_END), cos_b, sin_bm, sin_bp, MLA_ROPE // 2)
    lane = lax.broadcasted_iota(I32, misc.shape, 1)
    kpe = jnp.where((lane >= _MISC_KR) & (lane < _MISC_KR + MLA_ROPE), misc, 0.0)
    for hh in range(N_HEADS):
        sl = slice(hh * LANES, (hh + 1) * LANES)
        qa_ref[0, hh] = qa[:, sl].astype(BF16)
        ka_ref[0, hh] = (kn[:, sl] + kpe).astype(BF16)
    va = jnp.dot(ckv, wuv_ref[...], preferred_element_type=F32)
    vaT_ref[0] = _aug_transpose(va)

    qb = _rope(proj(_C_QB, _C_KB), cos_a, sin_am, sin_ap, DSA_ROT // 2) * (DSA_HEAD_DIM ** -0.5 * LOG2E)
    kb = _rope(proj(_C_KB, _C_VB), cos_a, sin_am, sin_ap, DSA_ROT // 2)
    qi = _rope(proj(_C_QI, _C_KI), cos_a, sin_am, sin_ap, DSA_ROT // 2) * (IDX_DIM ** -0.5)
    for j in range(N_HEADS // 2):
        sl = slice(j * LANES, (j + 1) * LANES)
        qb_ref[0, j] = qb[:, sl].astype(BF16)
        kb_ref[0, j] = kb[:, sl].astype(BF16)
        qi_ref[0, j] = qi[:, sl].astype(BF16)
    vbT_ref[0] = _aug_transpose(proj(_C_VB, _C_QI))
    ki2 = _rope(proj(_C_KI, _C_MISC), cos_a, sin_am, sin_ap, DSA_ROT // 2)
    ki_ref[0] = ki2.astype(BF16)
    wT_ref[0] = misc.T[_MISC_W:_MISC_W + IDX_HEADS, :] * (IDX_HEADS ** -0.5)


def _inproj_call(x, g_mix, w1, wuq, wuk, wuv, gq, gkv, tab, tm):
    b, s, d = x.shape
    nh, hp = N_HEADS, N_HEADS // 2
    full = lambda shape: pl.BlockSpec(shape, lambda bi, i: (0,) * len(shape))
    out_shape = (
        jax.ShapeDtypeStruct((b, nh, s, LANES), BF16),
        jax.ShapeDtypeStruct((b, nh, s, LANES), BF16),
        jax.ShapeDtypeStruct((b, nh * HEAD_VA, s), BF16),
        jax.ShapeDtypeStruct((b, hp, s, LANES), BF16),
        jax.ShapeDtypeStruct((b, hp, s, LANES), BF16),
        jax.ShapeDtypeStruct((b, nh * HEAD_VA, s), BF16),
        jax.ShapeDtypeStruct((b, hp, s, LANES), BF16),
        jax.ShapeDtypeStruct((b, s, LANES), BF16),
        jax.ShapeDtypeStruct((b, IDX_HEADS, s), F32),
    )
    head_spec = lambda n: pl.BlockSpec((1, n, tm, LANES), lambda bi, i: (bi, 0, i, 0))
    t_spec = pl.BlockSpec((1, nh * HEAD_VA, tm), lambda bi, i: (bi, 0, i))
    out_specs = (head_spec(nh), head_spec(nh), t_spec, head_spec(hp), head_spec(hp), t_spec,
                 head_spec(hp), pl.BlockSpec((1, tm, LANES), lambda bi, i: (bi, i, 0)),
                 pl.BlockSpec((1, IDX_HEADS, tm), lambda bi, i: (bi, 0, i)))
    in_specs = [
        pl.BlockSpec((1, tm, d), lambda bi, i: (bi, i, 0)),
        full(g_mix.shape), full(w1.shape), full(wuq.shape), full(wuk.shape), full(wuv.shape),
        full(gq.shape), full(gkv.shape),
        pl.BlockSpec((tab.shape[0], tm, LANES), lambda bi, i: (0, i, 0)),
    ]
    scale_a = (MLA_NOPE + MLA_ROPE) ** -0.5 * LOG2E
    return pl.pallas_call(
        functools.partial(_inproj_kernel, scale_a=scale_a),
        out_shape=out_shape, grid=(b, s // tm), in_specs=in_specs, out_specs=out_specs,
        compiler_params=pltpu.CompilerParams(
            dimension_semantics=("arbitrary", "arbitrary"), vmem_limit_bytes=VMEM_LIMIT),
        name="inproj",
    )(x, g_mix, w1, wuq, wuk, wuv, gq, gkv, tab)


def _head_scores(h, part, q_ref, k_ref, pairs):
    tq = q_ref.shape[2]
    qrows = slice(part * (tq // Q_SPLIT), (part + 1) * (tq // Q_SPLIT))
    if pairs:
        q = q_ref[0, h // 2, qrows, :]
        lane = lax.broadcasted_iota(I32, q.shape, 1)
        q = jnp.where(lax.shift_right_logical(lane, 6) == (h % 2), q, jnp.zeros_like(q))
        k = k_ref[0, h // 2]
    else:
        q = q_ref[0, h, qrows, :]
        k = k_ref[0, h]
    return lax.dot_general(k, q, (((1,), (1,)), ((), ())), preferred_element_type=F32)


def _flash_heads(q_ref, k_ref, vT_ref, m_scr, acc_scr, pairs, mask, lead):
    units = [(h, part) for h in range(N_HEADS) for part in range(Q_SPLIT)]
    pending = [_head_scores(h, part, q_ref, k_ref, pairs) for h, part in units[:lead]]
    for i, (h, part) in enumerate(units):
        if i + lead < len(units):
            hn, pn = units[i + lead]
            pending.append(_head_scores(hn, pn, q_ref, k_ref, pairs))
        _flash_update(h, part, pending.pop(0), vT_ref, m_scr, acc_scr, mask)


def _flash_update(h, part, s, vT_ref, m_scr, acc_scr, mask):
    w = s.shape[1]
    cols = slice(part * w, (part + 1) * w)
    if mask is not None:
        s = jnp.where(mask[:, cols], s, NEG)
    row = slice(h, h + 1)
    m_prev = m_scr[row, cols]
    m_new = jnp.maximum(m_prev, jnp.max(s, axis=0, keepdims=True))
    p = jnp.exp2(s - m_new).astype(BF16)
    alpha = jnp.exp2(m_prev - m_new)
    m_scr[row, cols] = m_new
    rows = slice(h * HEAD_VA, (h + 1) * HEAD_VA)
    pv = jnp.dot(vT_ref[0, rows, :], p, preferred_element_type=F32)
    acc_scr[rows, cols] = alpha * acc_scr[rows, cols] + pv


def _flash_init(m_scr, acc_scr):
    m_scr[...] = jnp.full(m_scr.shape, NEG, F32)
    acc_scr[...] = jnp.zeros(acc_scr.shape, F32)


def _flash_finalize(o_ref, acc_scr):
    parts = []
    for hh in range(N_HEADS):
        r0 = hh * HEAD_VA
        parts.append(acc_scr[r0:r0 + HEAD_V, :] * (1.0 / acc_scr[r0 + HEAD_V:r0 + HEAD_V + 1, :]))
    o_ref[0] = jnp.concatenate(parts, axis=0).T.astype(BF16)


def _causal_mask(qi, ki, tq, tk):
    s_idx = ki * tk + lax.broadcasted_iota(I32, (tk, tq), 0)
    t_idx = qi * tq + lax.broadcasted_iota(I32, (tk, tq), 1)
    return lax.shift_right_logical(s_idx, CHUNK_SHIFT) <= lax.shift_right_logical(t_idx, CHUNK_SHIFT)


def _kmax(qi, tq, tk):
    return ((qi + 1) * tq - 1) // tk


def _mla_kernel(qi_tab, ki_tab, q_ref, k_ref, vT_ref, o_ref, m_scr, acc_scr, *, tq, tk):
    p = pl.program_id(1)
    qi = qi_tab[p]
    ki = ki_tab[p]

    @pl.when(ki == 0)
    def _():
        _flash_init(m_scr, acc_scr)

    diag = (ki + 1) * tk > qi * tq

    def run(masked):
        mask = _causal_mask(qi, ki, tq, tk) if masked else None

        _flash_heads(q_ref, k_ref, vT_ref, m_scr, acc_scr, False, mask, MLA_QK_LEAD)

    pl.when(diag)(lambda: run(True))
    pl.when(jnp.logical_not(diag))(lambda: run(False))

    @pl.when(ki == _kmax(qi, tq, tk))
    def _():
        _flash_finalize(o_ref, acc_scr)


def _mla_call(qa, ka, vaT, tq, tk):
    b, nh, s, _ = qa.shape
    nq = s // tq
    qi_l, ki_l = [], []
    for qi in range(nq):
        for ki in range(((qi + 1) * tq - 1) // tk + 1):
            qi_l.append(qi)
            ki_l.append(ki)
    qi_tab = jnp.asarray(np.array(qi_l, np.int32))
    ki_tab = jnp.asarray(np.array(ki_l, np.int32))
    grid_spec = pltpu.PrefetchScalarGridSpec(
        num_scalar_prefetch=2, grid=(b, len(qi_l)),
        in_specs=[
            pl.BlockSpec((1, nh, tq, LANES), lambda bi, p, qt, kt: (bi, 0, qt[p], 0)),
            pl.BlockSpec((1, nh, tk, LANES), lambda bi, p, qt, kt: (bi, 0, kt[p], 0)),
            pl.BlockSpec((1, nh * HEAD_VA, tk), lambda bi, p, qt, kt: (bi, 0, kt[p])),
        ],
        out_specs=pl.BlockSpec((1, tq, nh * HEAD_V), lambda bi, p, qt, kt: (bi, qt[p], 0)),
        scratch_shapes=[pltpu.VMEM((nh, tq), F32), pltpu.VMEM((nh * HEAD_VA, tq), F32)],
    )
    return pl.pallas_call(
        functools.partial(_mla_kernel, tq=tq, tk=tk),
        out_shape=jax.ShapeDtypeStruct((b, s, nh * HEAD_V), BF16),
        grid_spec=grid_spec,
        compiler_params=pltpu.CompilerParams(
            dimension_semantics=("arbitrary", "arbitrary"), vmem_limit_bytes=VMEM_LIMIT),
        name="mla_attn",
    )(qi_tab, ki_tab, qa, ka, vaT)


_F_LAST_A, _F_PHASE_C, _F_FIRST_C, _F_LAST_C = 1, 2, 4, 8


KEY_LOWEST_FINITE = -2139095040


def _key_to_f32(k):
    return lax.bitcast_convert_type(k ^ (lax.shift_right_arithmetic(k, 31) & 0x7FFFFFFF), F32)


def _trunc_bf16(x):
    hi = lax.bitcast_convert_type(x, I32) & jnp.int32(-65536)
    return lax.bitcast_convert_type(hi, F32).astype(BF16)


def _count(sc_scr, nkb, trial, tq, tk, strict=False):
    def body(j, acc):
        sc = sc_scr[j]
        hit = ((sc > trial) if strict else (sc >= trial)).astype(I32)
        return acc + jnp.sum(hit.reshape(tk // 8, 8, tq), axis=0)
    acc = lax.fori_loop(0, nkb, body, jnp.zeros((8, tq), I32))
    return jnp.sum(acc, axis=0, keepdims=True)


def _min_above(sc_scr, nkb, bound, tq, tk, inclusive):
    def body(j, acc):
        sc = sc_scr[j]
        val = jnp.where((sc >= bound) if inclusive else (sc > bound), sc, jnp.inf)
        return jnp.minimum(acc, jnp.min(val.reshape(tk // 8, 8, tq), axis=0))
    acc = lax.fori_loop(0, nkb, body, jnp.full((8, tq), jnp.inf, F32))
    return jnp.min(acc, axis=0, keepdims=True)


def _count16(t16_scr, nkb, trial16, tq, tk):
    one = jnp.ones((), BF16)
    zero = jnp.zeros((), BF16)

    def body(j, acc):
        hit = jnp.where(t16_scr[j] >= trial16, one, zero)
        part = hit[0:16, :]
        for r in range(1, tk // 16):
            part = part + hit[r * 16:(r + 1) * 16, :]
        return acc + part.astype(F32)
    acc = lax.fori_loop(0, nkb, body, jnp.zeros((16, tq), F32))
    return jnp.sum(acc, axis=0, keepdims=True).astype(I32)


def _count_tied_before(sc_scr, nkb, thr, bound, tq, tk):
    def body(j, acc):
        idx = j * tk + lax.broadcasted_iota(I32, (tk, tq), 0)
        hit = jnp.where(sc_scr[j] == thr, (idx < bound).astype(I32), 0)
        return acc + jnp.sum(hit.reshape(tk // 8, 8, tq), axis=0)
    acc = lax.fori_loop(0, nkb, body, jnp.zeros((8, tq), I32))
    return jnp.sum(acc, axis=0, keepdims=True)


def _select_threshold(sc_scr, t16_scr, nkb, cand_scr, cnt_scr, thr_scr, top_k, s_len, tq, tk):
    cand_scr[...] = jnp.full((1, tq), INT_MIN, I32)
    cnt_scr[...] = jnp.full((1, tq), top_k + 1, I32)

    def make_step(counter):
        def step(c):
            bit, _ = c
            cand = cand_scr[...]
            trial = cand + lax.shift_left(jnp.int32(1), bit)
            cnt = counter(trial)
            ok = cnt >= top_k
            cand_scr[...] = jnp.where(ok, trial, cand)
            new_cnt = jnp.where(ok, cnt, cnt_scr[...])
            cnt_scr[...] = new_cnt
            pending = jnp.max(jnp.where(new_cnt == top_k, 0.0, 1.0))
            return bit - 1, (pending == 0.0).astype(I32)
        return step

    lax.while_loop(lambda c: c[0] >= 16,
                   make_step(lambda k: _count16(t16_scr, nkb, _trunc_bf16(_key_to_f32(k)), tq, tk)),
                   (jnp.int32(31), jnp.int32(0)))
    c16 = cand_scr[...]
    kp = jnp.where(c16 >= 0, c16, c16 | 0xFFFF)
    cand_scr[...] = kp - 0x8002
    cnt_scr[...] = jnp.full((1, tq), top_k + 1, I32)
    lax.while_loop(lambda c: jnp.logical_and(c[0] >= 0, c[1] == 0),
                   make_step(lambda k: _count(sc_scr, nkb, _key_to_f32(k), tq, tk)),
                   (jnp.int32(16), jnp.int32(0)))

    cand = cand_scr[...]
    thr_grid = _key_to_f32(jnp.maximum(cand, KEY_LOWEST_FINITE))
    thr_scr[...] = thr_grid
    inexact = jnp.logical_and(cnt_scr[...] != top_k, cand >= KEY_LOWEST_FINITE)

    @pl.when(jnp.max(inexact.astype(I32)) > 0)
    def _():
        def still_above(kth, active):
            more = jnp.logical_and(active > 0.0, _count(sc_scr, nkb, kth, tq, tk, strict=True) >= top_k)
            return more.astype(F32)

        def walk(c):
            kth, active, _ = c
            kth = jnp.where(active > 0.0, _min_above(sc_scr, nkb, kth, tq, tk, False), kth)
            active = still_above(kth, active)
            return kth, active, (jnp.max(active) > 0.0).astype(I32)
        kth0 = _min_above(sc_scr, nkb, thr_grid, tq, tk, True)
        act0 = still_above(kth0, inexact.astype(F32))
        kth, _, _ = lax.while_loop(lambda c: c[2] > 0, walk,
                                   (kth0, act0, (jnp.max(act0) > 0.0).astype(I32)))
        thr = jnp.where(inexact, kth, thr_grid)
        thr_scr[...] = thr
        need = top_k - _count(sc_scr, nkb, thr, tq, tk, strict=True)
        cnt_scr[...] = jnp.zeros((1, tq), I32)
        nbits = max(1, int(math.ceil(math.log2(s_len))))

        def jstep(i, c):
            bit = nbits - 1 - i
            j0 = cnt_scr[...]
            trial = j0 + lax.shift_left(jnp.int32(1), bit)
            below = _count_tied_before(sc_scr, nkb, thr, trial, tq, tk)
            cnt_scr[...] = jnp.where(below < need, trial, j0)
            return c
        lax.fori_loop(0, nbits, jstep, 0)
        j0 = jnp.where(inexact, cnt_scr[...], s_len)

        def demote(j, c):
            idx = j * tk + lax.broadcasted_iota(I32, (tk, tq), 0)
            sj = sc_scr[j]
            sc_scr[j] = jnp.where(jnp.logical_and(sj == thr, idx > j0), -jnp.inf, sj)
            return c
        lax.fori_loop(0, nkb, demote, 0)


def _dsa_kernel(qi_tab, ki_tab, kia_tab, kic_tab, fl_tab,
                qi_ref, ki_ref, wT_ref, qb_ref, kb_ref, vbT_ref, o_ref,
                sc_scr, t16_scr, cand_scr, cnt_scr, thr_scr, m_scr, acc_scr,
                *, tq, tk, top_k, s_len):
    p = pl.program_id(1)
    qi = qi_tab[p]
    ki = ki_tab[p]
    fl = fl_tab[p]

    @pl.when((fl & _F_PHASE_C) == 0)
    def _():
        kidx = ki_ref[0]

        score = None
        for h in range(IDX_HEADS):
            q = qi_ref[0, h // 2]
            lane = lax.broadcasted_iota(I32, q.shape, 1)
            q = jnp.where(lax.shift_right_logical(lane, 6) == (h % 2), q, jnp.zeros_like(q))
            lg = lax.dot_general(kidx, q, (((1,), (1,)), ((), ())), preferred_element_type=F32)
            t = wT_ref[0, h:h + 1, :] * jnp.maximum(lg, 0.0)
            score = t if score is None else score + t
        sc_scr[ki] = score
        t16_scr[ki] = score.astype(BF16)

        @pl.when((ki + 1) * tk > qi * tq)
        def _():
            masked = jnp.where(_causal_mask(qi, ki, tq, tk), sc_scr[ki], -jnp.inf)
            sc_scr[ki] = masked
            t16_scr[ki] = masked.astype(BF16)

        @pl.when((fl & _F_LAST_A) != 0)
        def _():
            _select_threshold(sc_scr, t16_scr, ki + 1, cand_scr, cnt_scr, thr_scr, top_k, s_len, tq, tk)

    @pl.when((fl & _F_PHASE_C) != 0)
    def _():
        @pl.when((fl & _F_FIRST_C) != 0)
        def _():
            _flash_init(m_scr, acc_scr)

        mask = sc_scr[ki] >= thr_scr[...]

        _flash_heads(qb_ref, kb_ref, vbT_ref, m_scr, acc_scr, True, mask, DSA_QK_LEAD)

        @pl.when((fl & _F_LAST_C) != 0)
        def _():
            _flash_finalize(o_ref, acc_scr)


def _dsa_call(qi_arr, ki_arr, wT, qb, kb, vbT, tq, tk, top_k):
    b, hp, s, _ = qb.shape
    nq, nkb = s // tq, s // tk
    qi_l, ki_l, kia_l, kic_l, fl_l = [], [], [], [], []
    for qi in range(nq):
        kmax = ((qi + 1) * tq - 1) // tk
        for ki in range(kmax + 1):
            qi_l.append(qi); ki_l.append(ki); kia_l.append(ki); kic_l.append(0)
            fl_l.append(_F_LAST_A if ki == kmax else 0)
        for ki in range(kmax + 1):
            qi_l.append(qi); ki_l.append(ki); kia_l.append(kmax); kic_l.append(ki)
            fl_l.append(_F_PHASE_C | (_F_FIRST_C if ki == 0 else 0) | (_F_LAST_C if ki == kmax else 0))
    tabs = [jnp.asarray(np.array(t, np.int32)) for t in (qi_l, ki_l, kia_l, kic_l, fl_l)]
    nh = N_HEADS
    grid_spec = pltpu.PrefetchScalarGridSpec(
        num_scalar_prefetch=5, grid=(b, len(qi_l)),
        in_specs=[
            pl.BlockSpec((1, hp, tq, LANES), lambda bi, p, qt, kt, ka, kc, f: (bi, 0, qt[p], 0)),
            pl.BlockSpec((1, tk, LANES), lambda bi, p, qt, kt, ka, kc, f: (bi, ka[p], 0)),
            pl.BlockSpec((1, IDX_HEADS, tq), lambda bi, p, qt, kt, ka, kc, f: (bi, 0, qt[p])),
            pl.BlockSpec((1, hp, tq, LANES), lambda bi, p, qt, kt, ka, kc, f: (bi, 0, qt[p], 0)),
            pl.BlockSpec((1, hp, tk, LANES), lambda bi, p, qt, kt, ka, kc, f: (bi, 0, kc[p], 0)),
            pl.BlockSpec((1, nh * HEAD_VA, tk), lambda bi, p, qt, kt, ka, kc, f: (bi, 0, kc[p])),
        ],
        out_specs=pl.BlockSpec((1, tq, nh * HEAD_V), lambda bi, p, qt, kt, ka, kc, f: (bi, qt[p], 0)),
        scratch_shapes=[
            pltpu.VMEM((nkb, tk, tq), F32),
            pltpu.VMEM((nkb, tk, tq), BF16),
            pltpu.VMEM((1, tq), I32), pltpu.VMEM((1, tq), I32), pltpu.VMEM((1, tq), F32),
            pltpu.VMEM((nh, tq), F32), pltpu.VMEM((nh * HEAD_VA, tq), F32),
        ],
    )
    return pl.pallas_call(
        functools.partial(_dsa_kernel, tq=tq, tk=tk, top_k=top_k, s_len=s),
        out_shape=jax.ShapeDtypeStruct((b, s, nh * HEAD_V), BF16),
        grid_spec=grid_spec,
        compiler_params=pltpu.CompilerParams(
            dimension_semantics=("arbitrary", "arbitrary"), vmem_limit_bytes=VMEM_LIMIT),
        name="dsa_attn",
    )(*tabs, qi_arr, ki_arr, wT, qb, kb, vbT)


_R_GROUP0, _R_EXP0 = 0, N_GROUPS


def _merge_kernel(x_ref, g_ref, wg_ref, oa_ref, ob_ref, woa_ref, wob_ref, wout_ref, gf_ref,
                  wr_ref, br_ref, x1_ref, h2_ref, comb_ref, gT_ref):
    x = x_ref[0]
    h = _rms(x, g_ref[...]).astype(BF16)
    gates = jnp.dot(h, wg_ref[...], preferred_element_type=F32)
    ya = jnp.dot(oa_ref[0], woa_ref[...], preferred_element_type=F32)
    yb = jnp.dot(ob_ref[0], wob_ref[...], preferred_element_type=F32)
    y = jax.nn.sigmoid(gates[:, :D_MODEL]) * ya + jax.nn.sigmoid(gates[:, D_MODEL:]) * yb
    x1 = x + jnp.dot(y.astype(BF16), wout_ref[...], preferred_element_type=F32)
    x1_ref[0] = x1
    h2 = _rms(x1, gf_ref[...])
    h2_ref[0] = h2.astype(BF16)

    h2_hi = h2.astype(BF16)
    h2_lo = (h2 - h2_hi.astype(F32)).astype(BF16)
    part = jnp.dot(h2_hi, wr_ref[...], preferred_element_type=F32)
    logits = (part[:, :LANES] + part[:, LANES:]
              + jnp.dot(h2_lo, wr_ref[:, :LANES], preferred_element_type=F32)) + br_ref[...]
    lane = lax.broadcasted_iota(I32, logits.shape, 1)
    ninf = -jnp.inf
    is_g = lane < N_GROUPS
    gl = jnp.where(is_g, logits, ninf)
    gmax = jnp.max(gl, axis=-1, keepdims=True)
    g_sel = jnp.min(jnp.where(gl == gmax, lane, LANES), axis=-1, keepdims=True)
    w_grp = 1.0 / jnp.sum(jnp.where(is_g, jnp.exp(gl - gmax), 0.0), axis=-1, keepdims=True)
    e_lane = lane - _R_EXP0
    in_grp = (e_lane >= 0) & (e_lane < N_EXPERTS) & (lax.shift_right_arithmetic(e_lane, 2) == g_sel)
    el = jnp.where(in_grp, logits, ninf)
    v1 = jnp.max(el, axis=-1, keepdims=True)
    i1 = jnp.min(jnp.where(el == v1, lane, LANES), axis=-1, keepdims=True)
    el2 = jnp.where(lane == i1, ninf, el)
    v2 = jnp.max(el2, axis=-1, keepdims=True)
    i2 = jnp.min(jnp.where(el2 == v2, lane, LANES), axis=-1, keepdims=True)
    e2 = jnp.exp(v2 - v1)
    w1 = 1.0 / (1.0 + e2)
    w2 = e2 * w1
    comb = jnp.where(lane == i1, w1 * w_grp, 0.0) + jnp.where(lane == i2, w2 * w_grp, 0.0)
    comb = comb + jnp.where(lane == 0, g_sel.astype(F32), 0.0)
    comb_ref[0] = comb
    gT_ref[0] = comb.T[0:8, :]


def _merge_call(x, g_mix, wg, oa, ob, woa, wob, wout, g_ffn, wr, br, tm):
    b, s, d = x.shape
    full = lambda a: pl.BlockSpec(a.shape, lambda bi, i: (0,) * a.ndim)
    tok = lambda w: pl.BlockSpec((1, tm, w), lambda bi, i: (bi, i, 0))
    return pl.pallas_call(
        _merge_kernel,
        out_shape=(jax.ShapeDtypeStruct((b, s, d), F32), jax.ShapeDtypeStruct((b, s, d), BF16),
                   jax.ShapeDtypeStruct((b, s, LANES), F32), jax.ShapeDtypeStruct((b, 8, s), F32)),
        grid=(b, s // tm),
        in_specs=[tok(d), full(g_mix), full(wg), tok(N_HEADS * HEAD_V), tok(N_HEADS * HEAD_V),
                  full(woa), full(wob), full(wout), full(g_ffn), full(wr), full(br)],
        out_specs=(tok(d), tok(d), tok(LANES), pl.BlockSpec((1, 8, tm), lambda bi, i: (bi, 0, i))),
        compiler_params=pltpu.CompilerParams(
            dimension_semantics=("arbitrary", "arbitrary"), vmem_limit_bytes=VMEM_LIMIT),
        name="merge_router",
    )(x, g_mix, wg, oa, ob, woa, wob, wout, g_ffn, wr, br)


def _moe_kernel(h2_ref, comb_ref, gT_ref, utri_ref, wgate_ref, wup_ref, wdown_ref, o_ref,
                xs_scr, cs_scr, ys_scr, pt_scr, meta_ref):
    i = pl.program_id(0)
    g = pl.program_id(1)
    tmo = h2_ref.shape[0]
    ns = xs_scr.shape[0]

    @pl.when(jnp.logical_and(i == 0, g == 0))
    def _():
        ys_scr[...] = jnp.zeros(ys_scr.shape, BF16)

    @pl.when(g == 0)
    def _():
        grow = gT_ref[0, 0:1, :]
        sub = lax.broadcasted_iota(I32, (16, tmo), 0).astype(F32)
        onehot = sub == grow
        ct = jnp.dot(jnp.where(onehot, 1.0, 0.0).astype(BF16), utri_ref[...],
                     preferred_element_type=F32)
        n_col = ct[:, tmo - 1:tmo]
        npad_col = jnp.floor((n_col + (MOE_SUB - 1)) * (1.0 / MOE_SUB)) * MOE_SUB
        bases, run = [], jnp.zeros((1, 1), F32)
        for r in range(N_GROUPS):
            bases.append(run)
            meta_ref[r] = run[0, 0].astype(I32)
            meta_ref[N_GROUPS + r] = (npad_col[r:r + 1, :] * (1.0 / MOE_SUB))[0, 0].astype(I32)
            run = run + npad_col[r:r + 1, :]
        meta_ref[2 * N_GROUPS] = run[0, 0].astype(I32)
        base_col = jnp.concatenate(bases + [jnp.zeros((16 - N_GROUPS, 1), F32)], axis=0)
        dest_row = jnp.sum(jnp.where(onehot, ct - 1.0 + base_col, 0.0), axis=0, keepdims=True)

        comb = comb_ref[...]
        c1 = comb.astype(BF16)
        c2 = (comb - c1.astype(F32)).astype(BF16)
        c3 = (comb - c1.astype(F32) - c2.astype(F32)).astype(BF16)
        rhs = jnp.concatenate([h2_ref[...], c1, c2, c3], axis=1)
        for cb in range(ns // MOE_CHUNK):
            @pl.when(cb * MOE_CHUNK < meta_ref[2 * N_GROUPS])
            def _():
                slot = (cb * MOE_CHUNK + lax.broadcasted_iota(I32, (MOE_CHUNK, tmo), 0)).astype(F32)
                p = jnp.where(slot == dest_row, 1.0, 0.0).astype(BF16)
                got = jnp.dot(p, rhs, preferred_element_type=F32)
                rows = slice(cb * MOE_CHUNK, (cb + 1) * MOE_CHUNK)
                xs_scr[rows, :] = got[:, :D_MODEL].astype(BF16)
                cs_scr[rows, :] = (got[:, D_MODEL:D_MODEL + LANES] + got[:, D_MODEL + LANES:D_MODEL + 2 * LANES]
                                   + got[:, D_MODEL + 2 * LANES:])
        dest_col = jnp.broadcast_to(dest_row, (LANES, tmo)).T
        for cb in range(ns // LANES):
            slot = (cb * LANES + lax.broadcasted_iota(I32, (tmo, LANES), 1)).astype(F32)
            pt_scr[:, cb * LANES:(cb + 1) * LANES] = jnp.where(slot == dest_col, 1.0, 0.0).astype(BF16)

    base = meta_ref[g]
    lane = lax.broadcasted_iota(I32, (MOE_SUB, LANES), 1)

    def block(j, c):
        rows = pl.ds(pl.multiple_of(base + j * MOE_SUB, MOE_SUB), MOE_SUB)
        x = xs_scr[rows, :]
        cw = cs_scr[rows, :]
        hids = []
        for e in range(EXPERTS_PER_GROUP):
            a = jnp.dot(x, wgate_ref[e], preferred_element_type=F32)
            u = jnp.dot(x, wup_ref[e], preferred_element_type=F32)
            ce = jnp.sum(jnp.where(lane == _R_EXP0 + g * EXPERTS_PER_GROUP + e, cw, 0.0),
                         axis=-1, keepdims=True)
            hids.append((a * jax.nn.sigmoid(a) * u * ce).astype(BF16))
        wd = wdown_ref[...].reshape(EXPERTS_PER_GROUP * EXPERT_FF, D_MODEL)
        y = jnp.dot(jnp.concatenate(hids, axis=1), wd, preferred_element_type=F32)
        ys_scr[rows, :] = y.astype(BF16)
        return c
    lax.fori_loop(0, meta_ref[N_GROUPS + g], block, 0)

    @pl.when(g == N_GROUPS - 1)
    def _():
        o_ref[...] = jnp.dot(pt_scr[...], ys_scr[...], preferred_element_type=F32).astype(BF16)


def _final_kernel(x1_ref, m_ref, g_ref, o_ref):
    o_ref[...] = _rms(x1_ref[...] + m_ref[...].astype(F32), g_ref[...])


def _moe_call(h2, comb, gT, wgate, wup, wdown, tmo):
    t, d = h2.shape
    epg = EXPERTS_PER_GROUP
    ns = tmo + N_GROUPS * MOE_SUB
    ns = -(-ns // MOE_CHUNK) * MOE_CHUNK
    utri = (np.arange(tmo)[:, None] <= np.arange(tmo)[None, :]).astype(np.float32)
    return pl.pallas_call(
        _moe_kernel,
        out_shape=jax.ShapeDtypeStruct((t, d), BF16),
        grid=(t // tmo, N_GROUPS),
        in_specs=[
            pl.BlockSpec((tmo, d), lambda i, g: (i, 0)),
            pl.BlockSpec((tmo, LANES), lambda i, g: (i, 0)),
            pl.BlockSpec((1, 8, tmo), lambda i, g: (0, 0, i)),
            pl.BlockSpec((tmo, tmo), lambda i, g: (0, 0)),
            pl.BlockSpec((epg, d, EXPERT_FF), lambda i, g: (g, 0, 0)),
            pl.BlockSpec((epg, d, EXPERT_FF), lambda i, g: (g, 0, 0)),
            pl.BlockSpec((epg, EXPERT_FF, d), lambda i, g: (g, 0, 0)),
        ],
        out_specs=pl.BlockSpec((tmo, d), lambda i, g: (i, 0)),
        scratch_shapes=[pltpu.VMEM((ns, d), BF16), pltpu.VMEM((ns, LANES), F32), pltpu.VMEM((ns, d), BF16),
                        pltpu.VMEM((tmo, ns), BF16), pltpu.SMEM((2 * N_GROUPS + 1,), I32)],
        compiler_params=pltpu.CompilerParams(
            dimension_semantics=("arbitrary", "arbitrary"), vmem_limit_bytes=VMEM_LIMIT),
        name="moe_sorted",
    )(h2, comb, gT.reshape(1, 8, t), jnp.asarray(utri, BF16), wgate, wup, wdown)


def _final_norm_tile(t):
    return min(512, t)


def _final_call(x1, m, g_fin, tm):
    t, d = x1.shape
    return pl.pallas_call(
        _final_kernel,
        out_shape=jax.ShapeDtypeStruct((t, d), F32),
        grid=(t // tm,),
        in_specs=[pl.BlockSpec((tm, d), lambda i: (i, 0)), pl.BlockSpec((tm, d), lambda i: (i, 0)),
                  pl.BlockSpec((1, d), lambda i: (0, 0))],
        out_specs=pl.BlockSpec((tm, d), lambda i: (i, 0)),
        compiler_params=pltpu.CompilerParams(dimension_semantics=("arbitrary",)),
        name="final_norm",
    )(x1, m, g_fin)


def _rope_tables(s):
    pos = jnp.arange(s, dtype=F32)[:, None]
    lane = np.arange(LANES)

    def build(rot, lane_in_rot, first, second):
        half = rot // 2
        inv_freq = ROPE_THETA ** (-jnp.arange(half, dtype=F32) * 2.0 / rot)
        ang = pos * inv_freq[None, :]
        cos_h, sin_h = jnp.cos(ang), jnp.sin(ang)
        idx = np.where(first | second, lane_in_rot % half, 0)
        cos = jnp.where(first | second, cos_h[:, idx], 1.0)
        sin_m = jnp.where(first, -sin_h[:, idx], 0.0)
        sin_p = jnp.where(second, sin_h[:, idx], 0.0)
        return [cos, sin_m, sin_p]

    l64 = lane % 64
    a = build(DSA_ROT, l64, l64 < DSA_ROT // 2, (l64 >= DSA_ROT // 2) & (l64 < DSA_ROT))
    lb = lane - _MISC_KR
    bt = build(MLA_ROPE, np.maximum(lb, 0), (lb >= 0) & (lb < MLA_ROPE // 2),
               (lb >= MLA_ROPE // 2) & (lb < MLA_ROPE))
    return jnp.stack(a + bt).astype(F32)


def _pad_heads(w, n_heads, width):
    k = w.shape[0]
    w = w.reshape(k, n_heads, width)
    return jnp.pad(w, ((0, 0), (0, 0), (0, LANES - width))).reshape(k, n_heads * LANES)


def _tile_sizes(s):
    tm = min(512, s)
    tq = min(512, s)
    tk = min(512, s)
    return tm, tq, tk


def kernel(x, norm_mix_g, w_in, mla_q_norm_g, w_uq, mla_kv_norm_g, w_uk, w_uv, w_o_a, w_o_b, w_out,
           norm_ffn_g, w_router_group, b_router_group, w_router_expert, b_router_expert,
           w_gate, w_up, w_down, final_norm_g):
    b, s, d = x.shape
    assert d == D_MODEL and s % CHUNK == 0
    tm, tq, tk = _tile_sizes(s)
    assert s % tm == 0 and s % tq == 0 and s % tk == 0 and tq % CHUNK == 0 and tk >= DSA_TOPK_MAX
    top_k = min(DSA_TOPK_MAX, s // 4)
    tab = _rope_tables(s)
    depth = w_in.shape[0]
    assert depth == 1, "the final norm is fused into the MoE kernel of the single layer"
    row = lambda v: v.reshape(1, -1).astype(F32)

    for l in range(depth):
        sizes = (MLA_Q_LORA, MLA_KV_LORA, MLA_ROPE, 512, 512, 512, 512, IDX_DIM, IDX_HEADS, D_MODEL, D_MODEL)
        offs = np.concatenate([[0], np.cumsum(sizes)])
        seg = [w_in[l][:, offs[i]:offs[i + 1]] for i in range(len(sizes))]
        (w_cq, w_ckv, w_kr, w_qb, w_kb, w_vb, w_qi, w_ki, w_wi, w_ga, w_gb) = seg
        misc = jnp.concatenate([jnp.zeros((d, _MISC_KR), F32), w_kr, w_wi,
                                jnp.zeros((d, LANES - _MISC_W - IDX_HEADS), F32)], axis=1)
        w1 = jnp.concatenate([w_cq, w_ckv, w_qb, w_kb, w_vb, w_qi, w_ki, w_ki, misc], axis=1).astype(BF16)
        wg = jnp.concatenate([w_ga, w_gb], axis=1).astype(BF16)
        wuq = _pad_heads(w_uq[l], N_HEADS, MLA_NOPE + MLA_ROPE).astype(BF16)
        wuk = _pad_heads(w_uk[l], N_HEADS, MLA_NOPE).astype(BF16)
        wuv = w_uv[l].astype(BF16)

        qa, ka, vaT, qb, kb, vbT, qi, ki, wT = _inproj_call(
            x, row(norm_mix_g[l]), w1, wuq, wuk, wuv, row(mla_q_norm_g[l]), row(mla_kv_norm_g[l]), tab, tm)
        o_a = _mla_call(qa, ka, vaT, tq, tk)
        o_b = _dsa_call(qi, ki, wT, qb, kb, vbT, tq, tk, top_k)

        wr = jnp.concatenate([w_router_group[l], w_router_expert[l],
                              jnp.zeros((d, LANES - N_GROUPS - N_EXPERTS), F32)], axis=1)
        wr_hi = wr.astype(BF16)
        wr = jnp.concatenate([wr_hi, (wr - wr_hi.astype(F32)).astype(BF16)], axis=1)
        br = jnp.concatenate([b_router_group[l], b_router_expert[l],
                              jnp.zeros((LANES - N_GROUPS - N_EXPERTS,), F32)]).reshape(1, LANES)
        x1, h2, comb, gT = _merge_call(x, row(norm_mix_g[l]), wg, o_a, o_b, w_o_a[l].astype(BF16),
                                       w_o_b[l].astype(BF16), w_out[l].astype(BF16), row(norm_ffn_g[l]), wr, br, tm)
        t = b * s
        tmo = min(MOE_TILE, t)
        assert t % tmo == 0
        moe = _moe_call(h2.reshape(t, d), comb.reshape(t, LANES), gT.transpose(1, 0, 2).reshape(8, t),
                        w_gate[l].astype(BF16), w_up[l].astype(BF16), w_down[l].astype(BF16), tmo)
        y = _final_call(x1.reshape(t, d), moe, row(final_norm_g), _final_norm_tile(t))
        x = y.reshape(b, s, d)
    return x
```
